```python
import jax, jax.numpy as jnp
from jax import lax
import numpy as np

D_MODEL = 1024
BATCH = 8
SEQ = 8192
DEPTH = 2
DEC_BATCH = 16
DEC_SEQ = 64
PAST_LEN = 2048

CHUNK = 64
Q_BLOCK = 128
D_MIX = D_MODEL
FOX_WIDTH = D_MIX // 2
FOX_HEAD_DIM = 64
FOX_HEADS = FOX_WIDTH // FOX_HEAD_DIM
HG_WIDTH = D_MIX - FOX_WIDTH
HG_HEAD_DIM = 128
HG_HEADS = HG_WIDTH // HG_HEAD_DIM
NORM_EPS = 1e-6
_SIZES = (FOX_WIDTH, FOX_WIDTH, FOX_WIDTH, FOX_HEADS, FOX_WIDTH, HG_WIDTH, HG_WIDTH, HG_WIDTH, HG_WIDTH)
D_IN = 4 * FOX_WIDTH + FOX_HEADS + 4 * HG_WIDTH
SPLIT_POINTS = tuple(int(v) for v in np.cumsum(_SIZES)[:-1])

kernel_name = "fox_hgrn2_parallel_stream_step"

F32 = jnp.float32


def rmsnorm(x, g):
    xf = x.astype(F32)
    y = xf * lax.rsqrt(jnp.mean(xf * xf, axis=-1, keepdims=True) + NORM_EPS)
    return (y * g.astype(F32)).astype(x.dtype)


def project(h, w_in, b_f, lb):
    B, S = h.shape[0], h.shape[1]
    z = jnp.einsum('bsd,de->bse', h, w_in)
    fq, fk, fv, ff, fg, hq, hf, hi, hg = jnp.split(z, SPLIT_POINTS, axis=-1)
    fq = fq.reshape(B, S, FOX_HEADS, FOX_HEAD_DIM)
    fk = fk.reshape(B, S, FOX_HEADS, FOX_HEAD_DIM)
    fv = fv.reshape(B, S, FOX_HEADS, FOX_HEAD_DIM)
    fox_logf = jax.nn.log_sigmoid(ff.astype(F32) + b_f.astype(F32))
    zf = hf.astype(F32).reshape(B, S, HG_HEADS, HG_HEAD_DIM)
    lbr = lb.reshape(HG_HEADS, HG_HEAD_DIM)
    hg_logf = jnp.logaddexp(jnp.log(lbr), jnp.log1p(-lbr) + jax.nn.log_sigmoid(zf))
    hg_k = (1.0 - lbr) * jax.nn.sigmoid(-zf)
    hq = jax.nn.silu(hq).reshape(B, S, HG_HEADS, HG_HEAD_DIM)
    hi = hi.reshape(B, S, HG_HEADS, HG_HEAD_DIM)
    return fq, fk, fv, fox_logf, fg, hq, hg_logf, hg_k, hi, hg


def fox_attend(q, k, v, c_q, c_k, q_pos, k_pos):
    s = jnp.einsum('bqhd,bkhd->bhqk', q.astype(F32), k.astype(F32)) * (FOX_HEAD_DIM ** -0.5)
    s = s + (jnp.swapaxes(c_q, 1, 2)[..., :, None] - jnp.swapaxes(c_k, 1, 2)[..., None, :])
    mask = k_pos[None, :] <= q_pos[:, None]
    s = jnp.where(mask, s, -jnp.inf)
    p = jax.nn.softmax(s, axis=-1)
    return jnp.einsum('bhqk,bkhd->bqhd', p.astype(v.dtype), v)


def fox_prompt(q, k, v, logf):
    B, S = q.shape[0], q.shape[1]
    c = jnp.cumsum(logf, axis=1)
    pos = jnp.arange(S)

    def block(i):
        st = i * Q_BLOCK
        qb = lax.dynamic_slice_in_dim(q, st, Q_BLOCK, axis=1)
        cb = lax.dynamic_slice_in_dim(c, st, Q_BLOCK, axis=1)
        return fox_attend(qb, k, v, cb, c, st + jnp.arange(Q_BLOCK), pos)

    out = lax.map(block, jnp.arange(S // Q_BLOCK))
    return jnp.swapaxes(out, 0, 1).reshape(B, S, FOX_WIDTH)


def fox_sample(q, k, v, logf, ck, cv, clogf):
    B, T = q.shape[0], q.shape[1]
    P = ck.shape[1]
    kk = jnp.concatenate([ck.astype(k.dtype), k], axis=1)
    vv = jnp.concatenate([cv.astype(v.dtype), v], axis=1)
    c = jnp.cumsum(jnp.concatenate([clogf.astype(F32), logf], axis=1), axis=1)
    out = fox_attend(q, kk, vv, c[:, P:], c, P + jnp.arange(T), jnp.arange(P + T))
    return out.reshape(B, T, FOX_WIDTH)


def hgrn_chunk(S0, q, logf, k, v):
    q, k, v = q.astype(F32), k.astype(F32), v.astype(F32)
    L = q.shape[1]
    b = jnp.cumsum(logf, axis=1)
    o_inter = jnp.einsum('blhk,bhkv->blhv', q * jnp.exp(b), S0)
    causal = jnp.tril(jnp.ones((L, L), dtype=bool))
    diff = b[:, :, None] - b[:, None, :]
    decay = jnp.exp(jnp.where(causal[None, :, :, None, None], diff, -jnp.inf))
    A = jnp.einsum('bthk,btshk,bshk->bhts', q, decay, k)
    o = o_inter + jnp.einsum('bhts,bshv->bthv', A, v)
    b_last = b[:, -1]
    S_new = jnp.exp(b_last)[..., None] * S0 + jnp.einsum(
        'bshk,bshv->bhkv', jnp.exp(b_last[:, None] - b) * k, v)
    return o, S_new


def hgrn_prompt(q, logf, k, v):
    B, S = q.shape[0], q.shape[1]
    n_chunks = S // CHUNK

    def to_chunks(a):
        return jnp.swapaxes(a.reshape(B, n_chunks, CHUNK, *a.shape[2:]), 0, 1)

    def step(state, inp):
        o, s_new = hgrn_chunk(state, *inp)
        return s_new, o

    S0 = jnp.zeros((B, HG_HEADS, HG_HEAD_DIM, HG_HEAD_DIM), F32)
    S_fin, o = lax.scan(step, S0, (to_chunks(q), to_chunks(logf), to_chunks(k), to_chunks(v)))
    return jnp.swapaxes(o, 0, 1).reshape(B, S, HG_HEADS, HG_HEAD_DIM), S_fin


def merge(fox_o, fox_gate, hg_o, hg_gate, g_norm, w_out):
    B, S = fox_gate.shape[0], fox_gate.shape[1]
    dt = fox_gate.dtype
    fox_y = fox_o.reshape(B, S, FOX_WIDTH) * jax.nn.silu(fox_gate)
    hn = hg_o * lax.rsqrt(jnp.mean(hg_o * hg_o, axis=-1, keepdims=True) + NORM_EPS)
    hn = hn * g_norm.astype(F32).reshape(HG_HEADS, HG_HEAD_DIM)
    hg_y = hn.reshape(B, S, HG_WIDTH).astype(dt) * jax.nn.silu(hg_gate)
    return jnp.einsum('bse,ed->bsd', jnp.concatenate([fox_y, hg_y], axis=-1), w_out)


def setup_inputs(seed: int = 0) -> dict:
    key = jax.random.key(seed)
    ks = jax.random.split(key, 14)
    nrm = jax.random.normal
    return {
        "x_prompt": nrm(ks[0], (BATCH, SEQ, D_MODEL), F32),
        "x_sample": nrm(ks[1], (DEC_BATCH, DEC_SEQ, D_MODEL), F32),
        "cache_k": nrm(ks[2], (DEPTH, DEC_BATCH, PAST_LEN, FOX_HEADS, FOX_HEAD_DIM), F32),
        "cache_v": nrm(ks[3], (DEPTH, DEC_BATCH, PAST_LEN, FOX_HEADS, FOX_HEAD_DIM), F32),
        "cache_logf": jax.nn.log_sigmoid(2.0 + nrm(ks[4], (DEPTH, DEC_BATCH, PAST_LEN, FOX_HEADS), F32)),
        "state_hgrn": 0.5 * nrm(ks[5], (DEPTH, DEC_BATCH, HG_HEADS, HG_HEAD_DIM, HG_HEAD_DIM), F32),
        "norm_g": 1.0 + 0.01 * nrm(ks[6], (DEPTH, D_MODEL), F32),
        "w_in": nrm(ks[7], (DEPTH, D_MODEL, D_IN), F32) * (D_MODEL ** -0.5),
        "fox_b_f": 0.1 * nrm(ks[8], (DEPTH, FOX_HEADS), F32),
        "hg_lower": nrm(ks[9], (DEPTH, HG_WIDTH), F32),
        "hg_norm_g": 1.0 + 0.01 * nrm(ks[10], (DEPTH, HG_WIDTH), F32),
        "w_out": nrm(ks[11], (DEPTH, D_MIX, D_MODEL), F32) * (D_MIX ** -0.5),
        "final_g": 1.0 + 0.01 * nrm(ks[12], (D_MODEL,), F32),
    }


def reference(x_prompt, x_sample, cache_k, cache_v, cache_logf, state_hgrn,
              norm_g, w_in, fox_b_f, hg_lower, hg_norm_g, w_out, final_g):
    lb_all = jnp.cumsum(jax.nn.softmax(hg_lower.astype(F32), axis=0), axis=0)

    xp = x_prompt
    kp, vp, lfp, sp = [], [], [], []
    for l in range(DEPTH):
        lb = lb_all[l] - lb_all[0]
        h = rmsnorm(xp, norm_g[l])
        fq, fk, fv, flogf, fg, hq, hlogf, hk, hi, hg = project(h, w_in[l], fox_b_f[l], lb)
        fox_o = fox_prompt(fq, fk, fv, flogf)
        hg_o, s_fin = hgrn_prompt(hq, hlogf, hk, hi)
        xp = xp + merge(fox_o, fg, hg_o, hg, hg_norm_g[l], w_out[l])
        kp.append(fk); vp.append(fv); lfp.append(flogf); sp.append(s_fin)
    y_prompt = rmsnorm(xp, final_g)

    xs = x_sample
    ksm, vsm, lfs, ss = [], [], [], []
    for l in range(DEPTH):
        lb = lb_all[l] - lb_all[0]
        h = rmsnorm(xs, norm_g[l])
        fq, fk, fv, flogf, fg, hq, hlogf, hk, hi, hg = project(h, w_in[l], fox_b_f[l], lb)
        fox_o = fox_sample(fq, fk, fv, flogf, cache_k[l], cache_v[l], cache_logf[l])
        hg_o, s_new = hgrn_chunk(state_hgrn[l].astype(F32), hq, hlogf, hk, hi)
        xs = xs + merge(fox_o, fg, hg_o, hg, hg_norm_g[l], w_out[l])
        ksm.append(fk); vsm.append(fv); lfs.append(flogf); ss.append(s_new)
    y_sample = rmsnorm(xs, final_g)

    return (y_prompt, y_sample,
            jnp.stack(kp), jnp.stack(vp), jnp.stack(lfp), jnp.stack(sp),
            jnp.stack(ksm), jnp.stack(vsm), jnp.stack(lfs), jnp.stack(ss))
```

```python
import functools

import jax
import jax.numpy as jnp
import numpy as np
from jax import lax
from jax.experimental import pallas as pl
from jax.experimental.pallas import tpu as pltpu

F32 = jnp.float32
BF16 = jnp.bfloat16

NORM_EPS = 1e-6
HG_CHUNK = 64
HG_SAFE_LOG_RANGE = 60.0
LANES = 128
NEG_BIG = -1e30
VMEM_LIMIT = 56 * 1024 * 1024

PROJ_TM = 512
ATT_T = 256
HG_TT = 256


def _cparams(n_axes):
    return pltpu.CompilerParams(dimension_semantics=("arbitrary",) * n_axes,
                                vmem_limit_bytes=VMEM_LIMIT)


def _log_sigmoid(x):
    return jnp.minimum(x, 0.0) - jnp.log1p(jnp.exp(-jnp.abs(x)))


def _silu(x):
    return x / (1.0 + jnp.exp(-x))


def _split3(a):
    hi = a.astype(BF16)
    r = a - hi.astype(F32)
    mid = r.astype(BF16)
    lo = (r - mid.astype(F32)).astype(BF16)
    return hi, mid, lo


def _dot(a, b):
    return jnp.dot(a, b, preferred_element_type=F32)


def _dot_nt(a, b):
    return lax.dot_general(a, b, (((1,), (1,)), ((), ())), preferred_element_type=F32)


def _proj_body(layer, fw, scale, x_ref, g_ref, wm_ref, wf_ref, bf_ref, hl_ref,
               q_ref, k_ref, v_ref, lf_ref, fg_ref, hq_ref, hlf_ref, hk_ref, hi_ref, hg_ref):
    x = x_ref[...]
    inv = lax.rsqrt(jnp.mean(x * x, axis=-1, keepdims=True) + NORM_EPS)
    hb = ((x * inv) * g_ref[...]).astype(BF16)

    def col(i):
        return _dot(hb, wm_ref[:, i * fw:(i + 1) * fw])

    q_ref[...] = (col(0) * scale).astype(BF16)
    k_ref[...] = col(1)
    v_ref[...] = col(2)
    lf_ref[...] = _log_sigmoid(_dot(hb, wf_ref[...]) + bf_ref[...])
    fg_ref[...] = col(3).astype(BF16)
    hq_ref[...] = _silu(col(4)).astype(BF16)

    rows = [hl_ref[i:i + 1, :] for i in range(hl_ref.shape[0])]
    mx = functools.reduce(jnp.maximum, rows)
    es = [jnp.exp(r - mx) for r in rows]
    tot = functools.reduce(lambda a, b: a + b, es)
    lb = jnp.zeros_like(mx)
    for i in range(1, layer + 1):
        lb = lb + es[i] / tot

    zf = col(5)
    a = jnp.log(lb)
    b = jnp.log1p(-lb) + _log_sigmoid(zf)
    hlf_ref[...] = jnp.maximum(a, b) + jnp.log1p(jnp.exp(-jnp.abs(a - b)))
    hk_ref[...] = ((1.0 - lb) / (1.0 + jnp.exp(zf))).astype(BF16)
    hi_ref[...] = col(6).astype(BF16)
    hg_ref[...] = col(7).astype(BF16)


def _proj(x2d, g, wm, wf, bfp, hl, layer, scale):
    n, d = x2d.shape
    fw = wm.shape[1] // 8
    tm = PROJ_TM
    row = lambda i: (i, 0)
    const = lambda i: (0, 0)
    f32o = jax.ShapeDtypeStruct((n, fw), F32)
    b16o = jax.ShapeDtypeStruct((n, fw), BF16)
    blk = pl.BlockSpec((tm, fw), row)
    return pl.pallas_call(
        functools.partial(_proj_body, layer, fw, scale),
        grid=(n // tm,),
        in_specs=[pl.BlockSpec((tm, d), row), pl.BlockSpec((1, d), const),
                  pl.BlockSpec(wm.shape, const), pl.BlockSpec(wf.shape, const),
                  pl.BlockSpec(bfp.shape, const), pl.BlockSpec(hl.shape, const)],
        out_specs=[blk, blk, blk, pl.BlockSpec((tm, LANES), row), blk, blk, blk, blk, blk, blk],
        out_shape=[b16o, f32o, f32o, jax.ShapeDtypeStruct((n, LANES), F32),
                   b16o, b16o, f32o, b16o, b16o, b16o],
        compiler_params=_cparams(1),
        name="proj",
    )(x2d, g, wm, wf, bfp, hl)


def _prep_body(nh, hd, q_ref, k_ref, v_ref, lf_ref, tri_ref, pk_ref, plq_ref, plk_ref,
               cq_ref, ck_ref, qT_ref, kp_ref, vT_ref, carry_ref):
    @pl.when(pl.program_id(1) == 0)
    def _():
        carry_ref[...] = jnp.zeros_like(carry_ref)

    ts = lf_ref.shape[0]
    tri = tri_ref[...]
    c = carry_ref[0:1, :]
    for part in _split3(lf_ref[...]):
        c = c + _dot(tri, part)
    carry_ref[0:1, :] = c[ts - 1:ts, :]

    ch = _split3(c)
    pk = pk_ref[...]
    kp = _dot(k_ref[...].astype(BF16), pk) + ck_ref[...]
    qp = _dot(q_ref[...], pk) + cq_ref[...]
    for i in range(3):
        kp = kp + _dot(ch[i], plk_ref[i])
        qp = qp + _dot(ch[i], plq_ref[i])
    kp_ref[...] = kp.astype(BF16)
    for h in range(nh):
        qT_ref[h] = qp[:, LANES * h:LANES * (h + 1)].T.astype(BF16)
    vt = v_ref[...].T
    for h in range(nh):
        vT_ref[h] = vt[hd * h:hd * (h + 1), :].astype(BF16)


def _prep_constants(nh, hd, ts):
    fw = nh * hd
    pw = nh * LANES
    tri = np.tril(np.ones((ts, ts), np.float32))
    pk = np.zeros((fw, pw), np.float32)
    plq = np.zeros((3, LANES, pw), np.float32)
    plk = np.zeros((3, LANES, pw), np.float32)
    cq = np.zeros((1, pw), np.float32)
    ck = np.zeros((1, pw), np.float32)
    for h in range(nh):
        for dd in range(hd):
            pk[h * hd + dd, h * LANES + dd] = 1.0
        for i in range(3):
            plq[i, h, h * LANES + hd + i] = 1.0
            ck[0, h * LANES + hd + i] = 1.0
            plk[i, h, h * LANES + hd + 3 + i] = -1.0
            cq[0, h * LANES + hd + 3 + i] = 1.0
    return (jnp.asarray(tri, BF16), jnp.asarray(pk, BF16), jnp.asarray(plq, BF16),
            jnp.asarray(plk, BF16), jnp.asarray(cq), jnp.asarray(ck))


def _prep(q, k, v, lf, nh, hd):
    b, tk, fw = k.shape
    tq = q.shape[1]
    ts = ATT_T
    q_off = (tk - tq) // ts
    tri, pk, plq, plk, cq, ck = _prep_constants(nh, hd, ts)
    pw = nh * LANES
    seq = lambda bi, s: (bi, s, 0)
    qidx = lambda bi, s: (bi, jnp.maximum(s - q_off, 0), 0)
    c2 = lambda bi, s: (0, 0)
    c3 = lambda bi, s: (0, 0, 0)
    return pl.pallas_call(
        functools.partial(_prep_body, nh, hd),
        grid=(b, tk // ts),
        in_specs=[pl.BlockSpec((None, ts, fw), qidx), pl.BlockSpec((None, ts, fw), seq),
                  pl.BlockSpec((None, ts, fw), seq), pl.BlockSpec((None, ts, LANES), seq),
                  pl.BlockSpec(tri.shape, c2), pl.BlockSpec(pk.shape, c2),
                  pl.BlockSpec(plq.shape, c3), pl.BlockSpec(plk.shape, c3),
                  pl.BlockSpec(cq.shape, c2), pl.BlockSpec(ck.shape, c2)],
        out_specs=[pl.BlockSpec((None, nh, LANES, ts),
                                lambda bi, s: (bi, 0, 0, jnp.maximum(s - q_off, 0))),
                   pl.BlockSpec((None, ts, pw), seq),
                   pl.BlockSpec((None, nh, None, hd, ts), lambda bi, s: (bi, 0, s, 0, 0))],
        out_shape=[jax.ShapeDtypeStruct((b, nh, LANES, tq), BF16),
                   jax.ShapeDtypeStruct((b, tk, pw), BF16),
                   jax.ShapeDtypeStruct((b, nh, tk // ts, hd, ts), BF16)],
        scratch_shapes=[pltpu.VMEM((8, LANES), F32)],
        compiler_params=_cparams(2),
        name="prep",
    )(q, k, v, lf, tri, pk, plq, plk, cq, ck)


def _attn_body(q_off, hd, qT_ref, kp_ref, vT_ref, o_ref):
    tq = qT_ref.shape[1]
    tk = vT_ref.shape[2]
    qT = qT_ref[...]
    ndiag = q_off + pl.program_id(1)

    def step(j, carry, masked):
        m, l, acc = carry
        kb = kp_ref[pl.ds(pl.multiple_of(j * tk, tk), tk), :]
        s = _dot(kb, qT)
        if masked:
            kpos = lax.broadcasted_iota(jnp.int32, (tk, tq), 0)
            qpos = lax.broadcasted_iota(jnp.int32, (tk, tq), 1)
            s = jnp.where(kpos <= qpos, s, NEG_BIG)
        m_new = jnp.maximum(m, jnp.max(s, axis=0, keepdims=True))
        alpha = jnp.exp(m - m_new)
        p = jnp.exp(s - m_new)
        l = alpha * l + jnp.sum(p, axis=0, keepdims=True)
        acc = alpha * acc + _dot(vT_ref[j], p.astype(BF16))
        return m_new, l, acc

    init = (jnp.full((1, tq), NEG_BIG, F32), jnp.zeros((1, tq), F32), jnp.zeros((hd, tq), F32))
    carry = lax.fori_loop(0, ndiag, lambda j, c: step(j, c, False), init)
    _, l, acc = step(ndiag, carry, True)
    o_ref[...] = acc / l


def _attn(qT, kp, vT):
    b, nh, _, tq_all = qT.shape
    tk_all = kp.shape[1]
    hd = vT.shape[3]
    t = ATT_T
    q_off = (tk_all - tq_all) // t
    return pl.pallas_call(
        functools.partial(_attn_body, q_off, hd),
        grid=(b * nh, tq_all // t),
        in_specs=[pl.BlockSpec((None, None, LANES, t), lambda g, i: (g // nh, g % nh, 0, i)),
                  pl.BlockSpec((None, tk_all, LANES), lambda g, i: (g // nh, 0, g % nh)),
                  pl.BlockSpec((None, None, tk_all // t, hd, t),
                               lambda g, i: (g // nh, g % nh, 0, 0, 0))],
        out_specs=pl.BlockSpec((None, hd, t), lambda g, i: (g // nh, g % nh, i)),
        out_shape=jax.ShapeDtypeStruct((b, nh * hd, tq_all), F32),
        compiler_params=_cparams(2),
        name="attn",
    )(qT, kp, vT)


def _hgrn_body(q_ref, lf_ref, k_ref, v_ref, s0_ref, tri_ref, o_ref, sfin_ref,
               st_ref, b_ref, qf_ref, kf_ref, oi_ref):
    tt, hd = lf_ref.shape
    cl = HG_CHUNK
    nchunk = tt // cl

    @pl.when(pl.program_id(2) == 0)
    def _():
        st_ref[...] = s0_ref[...].T

    tri = tri_ref[...]
    b = jnp.zeros((tt, hd), F32)
    for part in _split3(lf_ref[...]):
        b = b + _dot(tri, part)
    b_ref[...] = b
    qf_ref[...] = q_ref[...].astype(F32)
    kf_ref[...] = k_ref[...].astype(F32)

    worst = jnp.zeros((1, hd), F32)
    for c in range(nchunk):
        r = b[c * cl + cl // 2 - 1:c * cl + cl // 2, :]
        last = b[(c + 1) * cl - 1:(c + 1) * cl, :]
        worst = jnp.maximum(worst, jnp.maximum(-r, r - last))
    safe = jnp.max(worst) <= HG_SAFE_LOG_RANGE

    @pl.when(safe)
    def _():
        rid = lax.broadcasted_iota(jnp.int32, (cl, cl), 0)
        cid = lax.broadcasted_iota(jnp.int32, (cl, cl), 1)
        for c in range(nchunk):
            sl = slice(c * cl, (c + 1) * cl)
            bc = b_ref[sl, :]
            r = bc[cl // 2 - 1:cl // 2, :]
            qe = (qf_ref[sl, :] * jnp.exp(bc - r)).astype(BF16)
            ke = (kf_ref[sl, :] * jnp.exp(r - bc)).astype(BF16)
            a = jnp.where(rid >= cid, _dot_nt(qe, ke), 0.0)
            oi_ref[sl, :] = _dot(a.astype(BF16), v_ref[sl, :])

    @pl.when(jnp.logical_not(safe))
    def _():
        def row(t, _):
            base = pl.multiple_of((t // cl) * cl, cl)
            bt = b_ref[pl.ds(t, 1), :]
            qt = qf_ref[pl.ds(t, 1), :]
            bch = b_ref[pl.ds(base, cl), :]
            e = jnp.exp(jnp.minimum(bt - bch, 0.0))
            spos = base + lax.broadcasted_iota(jnp.int32, (cl, 1), 0)
            w = jnp.where(spos <= t, qt * kf_ref[pl.ds(base, cl), :] * e, 0.0)
            a = jnp.sum(w, axis=1, keepdims=True)
            vch = v_ref[pl.ds(base, cl), :].astype(F32)
            oi_ref[pl.ds(t, 1), :] = jnp.sum(a * vch, axis=0, keepdims=True)
            return 0
        lax.fori_loop(0, tt, row, 0)

    st = st_ref[...]
    zpad = jnp.zeros((LANES - cl, hd), F32)
    for c in range(nchunk):
        sl = slice(c * cl, (c + 1) * cl)
        bc = b_ref[sl, :]
        last = bc[cl - 1:cl, :]
        qb = (qf_ref[sl, :] * jnp.exp(bc)).astype(BF16)
        o_ref[sl, :] = oi_ref[sl, :] + _dot_nt(qb, st.astype(BF16))
        ke2 = jnp.concatenate([kf_ref[sl, :] * jnp.exp(last - bc), zpad], axis=0).astype(BF16)
        vt = jnp.concatenate([v_ref[sl, :].astype(F32), zpad], axis=0).T.astype(BF16)
        st = st * jnp.exp(last) + _dot(vt, ke2)
    st_ref[...] = st

    @pl.when(pl.program_id(2) == pl.num_programs(2) - 1)
    def _():
        sfin_ref[...] = st.T


def _hgrn(hq, hlf, hk, hi, s0, t_valid):
    b, _, w = hq.shape
    nh, hd = s0.shape[1], s0.shape[2]
    tt = min(HG_TT, t_valid)
    cl = HG_CHUNK
    tri = jnp.asarray(np.kron(np.eye(tt // cl, dtype=np.float32),
                              np.tril(np.ones((cl, cl), np.float32))), BF16)
    seq = lambda bi, h, s: (bi, s, h)
    sidx = lambda bi, h, s: (bi, h, 0, 0)
    blk = pl.BlockSpec((None, tt, hd), seq)
    return pl.pallas_call(
        _hgrn_body,
        grid=(b, nh, t_valid // tt),
        in_specs=[blk, blk, blk, blk, pl.BlockSpec((None, None, hd, hd), sidx),
                  pl.BlockSpec(tri.shape, lambda bi, h, s: (0, 0))],
        out_specs=[blk, pl.BlockSpec((None, None, hd, hd), sidx)],
        out_shape=[jax.ShapeDtypeStruct((b, t_valid, w), F32),
                   jax.ShapeDtypeStruct((b, nh, hd, hd), F32)],
        scratch_shapes=[pltpu.VMEM((hd, hd), F32)] + [pltpu.VMEM((tt, hd), F32)] * 4,
        compiler_params=_cparams(3),
        name="hgrn",
    )(hq, hlf, hk, hi, s0, tri)


def _merge_body(final, nhg, x_ref, oT_ref, fg_ref, ho_ref, hg_ref, gn_ref, w1_ref, w2_ref, fgain_ref,
                y_ref):
    fy = (oT_ref[...].T * _silu(fg_ref[...].astype(F32))).astype(BF16)
    ho = ho_ref[...]
    hw = ho.shape[1] // nhg
    normed = []
    for h in range(nhg):
        oh = ho[:, h * hw:(h + 1) * hw]
        inv = lax.rsqrt(jnp.mean(oh * oh, axis=-1, keepdims=True) + NORM_EPS)
        normed.append(oh * inv * gn_ref[:, h * hw:(h + 1) * hw])
    hy = (jnp.concatenate(normed, axis=1) * _silu(hg_ref[...].astype(F32))).astype(BF16)
    out = x_ref[...] + _dot(fy, w1_ref[...]) + _dot(hy, w2_ref[...])
    if final:
        inv = lax.rsqrt(jnp.mean(out * out, axis=-1, keepdims=True) + NORM_EPS)
        out = (out * inv) * fgain_ref[...]
    y_ref[...] = out


def _merge(x, oT, fg, ho, hg, gn, w1, w2, fgain, nhg, final):
    b, t, d = x.shape
    fw = fg.shape[2]
    tm = min(PROJ_TM, t)
    seq = lambda bi, s: (bi, s, 0)
    const = lambda bi, s: (0, 0)
    blk = pl.BlockSpec((None, tm, fw), seq)
    return pl.pallas_call(
        functools.partial(_merge_body, final, nhg),
        grid=(b, t // tm),
        in_specs=[pl.BlockSpec((None, tm, d), seq),
                  pl.BlockSpec((None, fw, tm), lambda bi, s: (bi, 0, s)),
                  blk, blk, blk, pl.BlockSpec(gn.shape, const),
                  pl.BlockSpec(w1.shape, const), pl.BlockSpec(w2.shape, const),
                  pl.BlockSpec(fgain.shape, const)],
        out_specs=pl.BlockSpec((None, tm, d), seq),
        out_shape=jax.ShapeDtypeStruct((b, t, d), F32),
        compiler_params=_cparams(2),
        name="merge",
    )(x, oT, fg, ho, hg, gn, w1, w2, fgain)


def _layer(x, t_valid, cache, s0, wts, layer, final):
    b, t, d = x.shape
    nh, hd, nhg = wts["nh"], wts["hd"], wts["nhg"]
    fw = nh * hd
    q, k, v, lf, fg, hq, hlf, hk, hi, hg = _proj(
        x.reshape(b * t, d), wts["g"][layer], wts["wm"][layer], wts["wf"][layer],
        wts["bf"][layer], wts["hl"], layer, float(hd) ** -0.5)
    r3 = lambda a: a.reshape(b, t, a.shape[-1])
    q, k, v, lf, fg, hq, hlf, hk, hi, hg = map(r3, (q, k, v, lf, fg, hq, hlf, hk, hi, hg))
    if cache is None:
        k_all, v_all, lf_all = k, v, lf
    else:
        ck, cv, clf = cache
        k_all = jnp.concatenate([ck, k], axis=1)
        v_all = jnp.concatenate([cv, v], axis=1)
        lf_all = jnp.concatenate([jnp.pad(clf, ((0, 0), (0, 0), (0, LANES - nh))), lf], axis=1)
    qT, kp, vT = _prep(q, k_all, v_all, lf_all, nh, hd)
    oT = _attn(qT, kp, vT)
    ho, s_new = _hgrn(hq, hlf, hk, hi, s0, t_valid)
    if t_valid < t:
        ho = jnp.pad(ho, ((0, 0), (0, t - t_valid), (0, 0)))
    y = _merge(x, oT, fg, ho, hg, wts["gn"][layer], wts["w1"][layer], wts["w2"][layer],
               wts["fgain"], nhg, final)
    kv_shape = (b, t_valid, nh, hd)
    return (y, k[:, :t_valid].reshape(kv_shape), v[:, :t_valid].reshape(kv_shape),
            lf[:, :t_valid, :nh], s_new)


def kernel(x_prompt, x_sample, cache_k, cache_v, cache_logf, state_hgrn,
           norm_g, w_in, fox_b_f, hg_lower, hg_norm_g, w_out, final_g):
    depth, d = norm_g.shape
    nh = fox_b_f.shape[1]
    fw = w_out.shape[1] // 2
    hd = fw // nh
    nhg = state_hgrn.shape[2]
    o = 3 * fw
    wm = jnp.concatenate([w_in[:, :, :o], w_in[:, :, o + nh:]], axis=2).astype(BF16)
    wf = jnp.pad(w_in[:, :, o:o + nh], ((0, 0), (0, 0), (0, LANES - nh))).astype(BF16)
    wts = dict(
        nh=nh, hd=hd, nhg=nhg, wm=wm, wf=wf,
        g=norm_g.reshape(depth, 1, d),
        bf=jnp.pad(fox_b_f, ((0, 0), (0, LANES - nh))).reshape(depth, 1, LANES),
        hl=hg_lower, gn=hg_norm_g.reshape(depth, 1, -1),
        w1=w_out[:, :fw, :].astype(BF16), w2=w_out[:, fw:, :].astype(BF16),
        fgain=final_g.reshape(1, d))

    bp, sp, _ = x_prompt.shape
    xp = x_prompt
    outs_p = []
    zero_state = jnp.zeros((bp,) + state_hgrn.shape[2:], F32)
    for l in range(depth):
        xp, k, v, lf, s_new = _layer(xp, sp, None, zero_state, wts, l, l == depth - 1)
        outs_p.append((k, v, lf, s_new))

    bs, ts_, _ = x_sample.shape
    t_pad = -(-ts_ // ATT_T) * ATT_T
    xs = jnp.pad(x_sample, ((0, 0), (0, t_pad - ts_), (0, 0)))
    outs_s = []
    for l in range(depth):
        past = cache_k.shape[2]
        cache = (cache_k[l].reshape(bs, past, fw), cache_v[l].reshape(bs, past, fw), cache_logf[l])
        xs, k, v, lf, s_new = _layer(xs, ts_, cache, state_hgrn[l], wts, l, l == depth - 1)
        outs_s.append((k, v, lf, s_new))

    stack = lambda outs, i: jnp.stack([o_[i] for o_ in outs])
    return (xp, xs[:, :ts_],
            stack(outs_p, 0), stack(outs_p, 1), stack(outs_p, 2), stack(outs_p, 3),
            stack(outs_s, 0), stack(outs_s, 1), stack(outs_s, 2), stack(outs_s, 3))
```

```python
import functools

import jax
import jax.numpy as jnp
import numpy as np
from jax import lax
from jax.experimental import pallas as pl
from jax.experimental.pallas import tpu as pltpu

F32 = jnp.float32
BF16 = jnp.bfloat16

NORM_EPS = 1e-6
HG_CHUNK = 64
HG_SAFE_LOG_RANGE = 60.0
LANES = 128
NEG_BIG = -1e30
ATT_SKIP_LOG = 110.0
ATT_NORM_SLACK = 1.02
VMEM_LIMIT = 56 * 1024 * 1024

PROJ_TM = 512
ATT_T = 256
ATT_QBLOCKS = 2
HG_TT = 256

ST_CMAX, ST_CMIN, ST_KNORM, ST_QNORM = 0, 1, 2, 3


def _cparams(n_axes):
    return pltpu.CompilerParams(dimension_semantics=("arbitrary",) * n_axes,
                                vmem_limit_bytes=VMEM_LIMIT)


def _log_sigmoid(x):
    return jnp.minimum(x, 0.0) - jnp.log1p(jnp.exp(-jnp.abs(x)))


def _silu(x):
    return x / (1.0 + jnp.exp(-x))


def _split3(a):
    hi = a.astype(BF16)
    r = a - hi.astype(F32)
    mid = r.astype(BF16)
    lo = (r - mid.astype(F32)).astype(BF16)
    return hi, mid, lo


def _dot(a, b):
    return jnp.dot(a, b, preferred_element_type=F32)


def _dot_nt(a, b):
    return lax.dot_general(a, b, (((1,), (1,)), ((), ())), preferred_element_type=F32)


def _proj_body(layer, fw, scale, x_ref, g_ref, wm_ref, wf_ref, bf_ref, hl_ref,
               q_ref, k_ref, v_ref, lf_ref, fg_ref, hq_ref, hlf_ref, hk_ref, hi_ref, hg_ref):
    x = x_ref[...]
    inv = lax.rsqrt(jnp.mean(x * x, axis=-1, keepdims=True) + NORM_EPS)
    hb = ((x * inv) * g_ref[...]).astype(BF16)

    def col(i):
        return _dot(hb, wm_ref[:, i * fw:(i + 1) * fw])

    q_ref[...] = (col(0) * scale).astype(BF16)
    k_ref[...] = col(1)
    v_ref[...] = col(2)
    lf_ref[...] = _log_sigmoid(_dot(hb, wf_ref[...]) + bf_ref[...])
    fg_ref[...] = col(3).astype(BF16)
    hq_ref[...] = _silu(col(4)).astype(BF16)

    rows = [hl_ref[i:i + 1, :] for i in range(hl_ref.shape[0])]
    mx = functools.reduce(jnp.maximum, rows)
    es = [jnp.exp(r - mx) for r in rows]
    tot = functools.reduce(lambda a, b: a + b, es)
    lb = jnp.zeros_like(mx)
    for i in range(1, layer + 1):
        lb = lb + es[i] / tot

    zf = col(5)
    a = jnp.log(lb)
    b = jnp.log1p(-lb) + _log_sigmoid(zf)
    hlf_ref[...] = jnp.maximum(a, b) + jnp.log1p(jnp.exp(-jnp.abs(a - b)))
    hk_ref[...] = ((1.0 - lb) / (1.0 + jnp.exp(zf))).astype(BF16)
    hi_ref[...] = col(6).astype(BF16)
    hg_ref[...] = col(7).astype(BF16)


def _proj(x2d, g, wm, wf, bfp, hl, layer, scale):
    n, d = x2d.shape
    fw = wm.shape[1] // 8
    tm = PROJ_TM
    row = lambda i: (i, 0)
    const = lambda i: (0, 0)
    f32o = jax.ShapeDtypeStruct((n, fw), F32)
    b16o = jax.ShapeDtypeStruct((n, fw), BF16)
    blk = pl.BlockSpec((tm, fw), row)
    return pl.pallas_call(
        functools.partial(_proj_body, layer, fw, scale),
        grid=(n // tm,),
        in_specs=[pl.BlockSpec((tm, d), row), pl.BlockSpec((1, d), const),
                  pl.BlockSpec(wm.shape, const), pl.BlockSpec(wf.shape, const),
                  pl.BlockSpec(bfp.shape, const), pl.BlockSpec(hl.shape, const)],
        out_specs=[blk, blk, blk, pl.BlockSpec((tm, LANES), row), blk, blk, blk, blk, blk, blk],
        out_shape=[b16o, f32o, f32o, jax.ShapeDtypeStruct((n, LANES), F32),
                   b16o, b16o, f32o, b16o, b16o, b16o],
        compiler_params=_cparams(1),
        name="proj",
    )(x2d, g, wm, wf, bfp, hl)


def _prep_body(nh, hd, n_past, *refs):
    if n_past:
        (q_ref, kc_ref, vc_ref, lfc_ref, kn_ref, vn_ref, lfn_ref, tri_ref, pk_ref, hsel_ref,
         plq_ref, plk_ref, cq_ref, ck_ref, qT_ref, kp_ref, vT_ref, st_ref, carry_ref) = refs
        is_new = pl.program_id(1) >= n_past
        k = jnp.where(is_new, kn_ref[...], kc_ref[...])
        v = jnp.where(is_new, vn_ref[...], vc_ref[...])
        lf = jnp.where(is_new, lfn_ref[...], lfc_ref[...])
    else:
        (q_ref, kn_ref, vn_ref, lfn_ref, tri_ref, pk_ref, hsel_ref,
         plq_ref, plk_ref, cq_ref, ck_ref, qT_ref, kp_ref, vT_ref, st_ref, carry_ref) = refs
        k, v, lf = kn_ref[...], vn_ref[...], lfn_ref[...]

    @pl.when(pl.program_id(1) == 0)
    def _():
        carry_ref[...] = jnp.zeros_like(carry_ref)

    ts = lf.shape[0]
    tri = tri_ref[...]
    c = carry_ref[0:1, :]
    for part in _split3(lf):
        c = c + _dot(tri, part)
    carry_ref[0:1, :] = c[ts - 1:ts, :]

    ch = _split3(c)
    pk = pk_ref[...]
    kb = k.astype(BF16)
    q = q_ref[...]
    kp = _dot(kb, pk) + ck_ref[...]
    qp = _dot(q, pk) + cq_ref[...]
    for i in range(3):
        kp = kp + _dot(ch[i], plk_ref[i])
        qp = qp + _dot(ch[i], plq_ref[i])
    kp_ref[...] = kp.astype(BF16)
    for h in range(nh):
        qT_ref[h] = qp[:, LANES * h:LANES * (h + 1)].T.astype(BF16)
    vt = v.T
    for h in range(nh):
        vT_ref[h] = vt[hd * h:hd * (h + 1), :].astype(BF16)

    hsel = hsel_ref[...]
    kf = kb.astype(F32)
    qf = q.astype(F32)
    kn = jnp.sqrt(jnp.max(_dot((kf * kf).astype(BF16), hsel), axis=0, keepdims=True))
    qn = jnp.sqrt(jnp.max(_dot((qf * qf).astype(BF16), hsel), axis=0, keepdims=True))
    st_ref[...] = jnp.concatenate(
        [jnp.max(c, axis=0, keepdims=True), jnp.min(c, axis=0, keepdims=True), kn, qn,
         jnp.zeros((4, LANES), F32)], axis=0)


def _prep_constants(nh, hd, ts):
    fw = nh * hd
    pw = nh * LANES
    tri = np.tril(np.ones((ts, ts), np.float32))
    pk = np.zeros((fw, pw), np.float32)
    hsel = np.zeros((fw, LANES), np.float32)
    plq = np.zeros((3, LANES, pw), np.float32)
    plk = np.zeros((3, LANES, pw), np.float32)
    cq = np.zeros((1, pw), np.float32)
    ck = np.zeros((1, pw), np.float32)
    for h in range(nh):
        for dd in range(hd):
            pk[h * hd + dd, h * LANES + dd] = 1.0
            hsel[h * hd + dd, h] = 1.0
        for i in range(3):
            plq[i, h, h * LANES + hd + i] = 1.0
            ck[0, h * LANES + hd + i] = 1.0
            plk[i, h, h * LANES + hd + 3 + i] = -1.0
            cq[0, h * LANES + hd + 3 + i] = 1.0
    return (jnp.asarray(tri, BF16), jnp.asarray(pk, BF16), jnp.asarray(hsel, BF16),
            jnp.asarray(plq, BF16), jnp.asarray(plk, BF16), jnp.asarray(cq), jnp.asarray(ck))


def _prep(q, new, cache, nh, hd):
    b, tq, fw = q.shape
    ts = ATT_T
    n_past = 0 if cache is None else cache[0].shape[1] // ts
    nblk = n_past + tq // ts
    consts = _prep_constants(nh, hd, ts)
    pw = nh * LANES
    seq_new = lambda bi, s: (bi, jnp.maximum(s - n_past, 0), 0)
    seq_old = lambda bi, s: (bi, jnp.minimum(s, n_past - 1), 0)
    seq = lambda bi, s: (bi, s, 0)

    def tok_specs(idx):
        return [pl.BlockSpec((None, ts, fw), idx), pl.BlockSpec((None, ts, fw), idx),
                pl.BlockSpec((None, ts, LANES), idx)]

    in_specs = [pl.BlockSpec((None, ts, fw), seq_new)]
    args = [q]
    if n_past:
        in_specs += tok_specs(seq_old)
        args += list(cache)
    in_specs += tok_specs(seq_new)
    args += list(new)
    for cst in consts:
        in_specs.append(pl.BlockSpec(cst.shape, lambda bi, s, nd=cst.ndim: (0,) * nd))
    args += list(consts)
    return pl.pallas_call(
        functools.partial(_prep_body, nh, hd, n_past),
        grid=(b, nblk),
        in_specs=in_specs,
        out_specs=[pl.BlockSpec((None, nh, LANES, ts),
                                lambda bi, s: (bi, 0, 0, jnp.maximum(s - n_past, 0))),
                   pl.BlockSpec((None, ts, pw), seq),
                   pl.BlockSpec((None, nh, None, hd, ts), lambda bi, s: (bi, 0, s, 0, 0)),
                   pl.BlockSpec((None, None, 8, LANES), lambda bi, s: (bi, s, 0, 0))],
        out_shape=[jax.ShapeDtypeStruct((b, nh, LANES, tq), BF16),
                   jax.ShapeDtypeStruct((b, nblk * ts, pw), BF16),
                   jax.ShapeDtypeStruct((b, nh, nblk, hd, ts), BF16),
                   jax.ShapeDtypeStruct((b, nblk, 8, LANES), F32)],
        scratch_shapes=[pltpu.VMEM((8, LANES), F32)],
        compiler_params=_cparams(2),
        name="prep",
    )(*args)


def _attn_body(q_off, hd, nqb, qT_ref, kp_ref, vT_ref, st_ref, o_ref):
    t = ATT_T
    nblk = st_ref.shape[1]
    st = st_ref[...]
    blk_id = lax.broadcasted_iota(jnp.int32, (1, nblk), 1)
    krow_minus_qcol = (lax.broadcasted_iota(jnp.int32, (2 * t, t), 0)
                       - lax.broadcasted_iota(jnp.int32, (2 * t, t), 1))

    def online(j, carry, qT):
        m, l, acc = carry
        kb = kp_ref[pl.ds(pl.multiple_of(j * t, t), t), :]
        s = _dot(kb, qT)
        m_new = jnp.maximum(m, jnp.max(s, axis=0, keepdims=True))
        alpha = jnp.exp(m - m_new)
        p = jnp.exp(s - m_new)
        l = alpha * l + jnp.sum(p, axis=0, keepdims=True)
        acc = alpha * acc + _dot(vT_ref[j], p.astype(BF16))
        return m_new, l, acc

    chains = []
    for u in range(nqb):
        qblk = q_off + pl.program_id(1) * nqb + u
        wstart = jnp.maximum(qblk - 1, 0)
        qT = qT_ref[:, u * t:(u + 1) * t]
        kb = kp_ref[pl.ds(pl.multiple_of(wstart * t, t), 2 * t), :]
        s = _dot(kb, qT)
        s = jnp.where(krow_minus_qcol <= (qblk - wstart) * t, s, NEG_BIG)
        m = jnp.max(s, axis=0, keepdims=True)
        p = jnp.exp(s - m)
        l = jnp.sum(p, axis=0, keepdims=True)
        pb = p.astype(BF16)
        acc = _dot(vT_ref[wstart], pb[:t, :]) + _dot(vT_ref[wstart + 1], pb[t:, :])

        sel = blk_id == qblk
        pick = lambda r: jnp.sum(jnp.where(sel, st[r:r + 1, :], 0.0), axis=1, keepdims=True)
        bound = (ATT_NORM_SLACK * pick(ST_QNORM) * (st[ST_KNORM:ST_KNORM + 1, :] + pick(ST_KNORM))
                 + (pick(ST_CMAX) - st[ST_CMIN:ST_CMIN + 1, :]))
        live = jnp.logical_and(bound >= -ATT_SKIP_LOG, blk_id < wstart)
        first = jnp.min(jnp.where(live, blk_id, wstart))
        chains.append((qT, wstart, first, (m, l, acc)))

    for u, (qT, wstart, first, carry) in enumerate(chains):
        _, l, acc = lax.fori_loop(first, wstart, lambda j, c, qT=qT: online(j, c, qT), carry)
        o_ref[:, u * t:(u + 1) * t] = acc / l


def _attn(qT, kp, vT, stats):
    b, nh, _, tq_all = qT.shape
    tk_all = kp.shape[1]
    hd = vT.shape[3]
    t = ATT_T
    nblk = tk_all // t
    q_off = (tk_all - tq_all) // t
    nqb = ATT_QBLOCKS if (tq_all // t) % ATT_QBLOCKS == 0 else 1
    tq = nqb * t
    return pl.pallas_call(
        functools.partial(_attn_body, q_off, hd, nqb),
        grid=(b * nh, tq_all // tq),
        in_specs=[pl.BlockSpec((None, None, LANES, tq), lambda g, i: (g // nh, g % nh, 0, i)),
                  pl.BlockSpec((None, tk_all, LANES), lambda g, i: (g // nh, 0, g % nh)),
                  pl.BlockSpec((None, None, nblk, hd, t), lambda g, i: (g // nh, g % nh, 0, 0, 0)),
                  pl.BlockSpec((None, None, 8, nblk), lambda g, i: (g // nh, g % nh, 0, 0))],
        out_specs=pl.BlockSpec((None, hd, tq), lambda g, i: (g // nh, g % nh, i)),
        out_shape=jax.ShapeDtypeStruct((b, nh * hd, tq_all), F32),
        compiler_params=_cparams(2),
        name="attn",
    )(qT, kp, vT, stats)


def _hgrn_body(nhg, q_ref, lf_ref, k_ref, v_ref, s0_ref, tri_ref, o_ref, sfin_ref,
               st_ref, b_ref, qf_ref, kf_ref, oi_ref):
    tt, w = lf_ref.shape
    hd = w // nhg
    cl = HG_CHUNK
    nchunk = tt // cl

    @pl.when(pl.program_id(1) == 0)
    def _():
        for h in range(nhg):
            st_ref[h] = s0_ref[h].T

    tri = tri_ref[...]
    b = jnp.zeros((tt, w), F32)
    for part in _split3(lf_ref[...]):
        b = b + _dot(tri, part)
    qf = q_ref[...].astype(F32)
    kf = k_ref[...].astype(F32)
    for h in range(nhg):
        hs = slice(h * hd, (h + 1) * hd)
        b_ref[h], qf_ref[h], kf_ref[h] = b[:, hs], qf[:, hs], kf[:, hs]

    rid = lax.broadcasted_iota(jnp.int32, (cl, cl), 0)
    cid = lax.broadcasted_iota(jnp.int32, (cl, cl), 1)
    zpad = jnp.zeros((LANES - cl, hd), F32)
    worst = jnp.zeros((1, hd), F32)
    for h in range(nhg):
        hs = slice(h * hd, (h + 1) * hd)
        st = st_ref[h]
        for c in range(nchunk):
            sl = slice(c * cl, (c + 1) * cl)
            bc, qc, kc = b[sl, hs], qf[sl, hs], kf[sl, hs]
            vc = v_ref[sl, hs]
            r = bc[cl // 2 - 1:cl // 2, :]
            last = bc[cl - 1:cl, :]
            worst = jnp.maximum(worst, jnp.maximum(-r, r - last))
            qe = (qc * jnp.exp(bc - r)).astype(BF16)
            ke = (kc * jnp.exp(r - bc)).astype(BF16)
            a = jnp.where(rid >= cid, _dot_nt(qe, ke), 0.0)
            oi_ref[h, sl, :] = _dot(a.astype(BF16), vc)
            qb = (qc * jnp.exp(bc)).astype(BF16)
            o_ref[sl, hs] = _dot_nt(qb, st.astype(BF16))
            ke2 = jnp.concatenate([kc * jnp.exp(last - bc), zpad], axis=0).astype(BF16)
            vt = jnp.concatenate([vc.astype(F32), zpad], axis=0).T.astype(BF16)
            st = st * jnp.exp(last) + _dot(vt, ke2)
        st_ref[h] = st

    def row(t, _):
        base = pl.multiple_of((t // cl) * cl, cl)
        spos = base + lax.broadcasted_iota(jnp.int32, (cl, 1), 0)
        for h in range(nhg):
            hs = slice(h * hd, (h + 1) * hd)
            bt = b_ref[h, pl.ds(t, 1), :]
            qt = qf_ref[h, pl.ds(t, 1), :]
            e = jnp.exp(jnp.minimum(bt - b_ref[h, pl.ds(base, cl), :], 0.0))
            wgt = jnp.where(spos <= t, qt * kf_ref[h, pl.ds(base, cl), :] * e, 0.0)
            a = jnp.sum(wgt, axis=1, keepdims=True)
            vch = v_ref[pl.ds(base, cl), hs].astype(F32)
            oi_ref[h, pl.ds(t, 1), :] = jnp.sum(a * vch, axis=0, keepdims=True)
        return 0

    unsafe_rows = jnp.where(jnp.max(worst) <= HG_SAFE_LOG_RANGE, 0, tt)
    lax.fori_loop(0, unsafe_rows, row, 0)
    for h in range(nhg):
        hs = slice(h * hd, (h + 1) * hd)
        o_ref[:, hs] = o_ref[:, hs] + oi_ref[h]

    @pl.when(pl.program_id(1) == pl.num_programs(1) - 1)
    def _():
        for h in range(nhg):
            sfin_ref[h] = st_ref[h].T


def _hgrn(hq, hlf, hk, hi, s0, t_valid):
    b, _, w = hq.shape
    nhg, hd = s0.shape[1], s0.shape[2]
    tt = min(HG_TT, t_valid)
    cl = HG_CHUNK
    tri = jnp.asarray(np.kron(np.eye(tt // cl, dtype=np.float32),
                              np.tril(np.ones((cl, cl), np.float32))), BF16)
    seq = lambda bi, s: (bi, s, 0)
    sidx = lambda bi, s: (bi, 0, 0, 0)
    blk = pl.BlockSpec((None, tt, w), seq)
    sblk = pl.BlockSpec((None, nhg, hd, hd), sidx)
    return pl.pallas_call(
        functools.partial(_hgrn_body, nhg),
        grid=(b, t_valid // tt),
        in_specs=[blk, blk, blk, blk, sblk, pl.BlockSpec(tri.shape, lambda bi, s: (0, 0))],
        out_specs=[blk, sblk],
        out_shape=[jax.ShapeDtypeStruct((b, t_valid, w), F32),
                   jax.ShapeDtypeStruct((b, nhg, hd, hd), F32)],
        scratch_shapes=[pltpu.VMEM((nhg, hd, hd), F32)] + [pltpu.VMEM((nhg, tt, hd), F32)] * 4,
        compiler_params=_cparams(2),
        name="hgrn",
    )(hq, hlf, hk, hi, s0, tri)


def _merge_body(final, nhg, x_ref, oT_ref, fg_ref, ho_ref, hg_ref, gn_ref, w1_ref, w2_ref, fgain_ref,
                y_ref):
    fy = (oT_ref[...].T * _silu(fg_ref[...].astype(F32))).astype(BF16)
    ho = ho_ref[...]
    hw = ho.shape[1] // nhg
    normed = []
    for h in range(nhg):
        oh = ho[:, h * hw:(h + 1) * hw]
        inv = lax.rsqrt(jnp.mean(oh * oh, axis=-1, keepdims=True) + NORM_EPS)
        normed.append(oh * inv * gn_ref[:, h * hw:(h + 1) * hw])
    hy = (jnp.concatenate(normed, axis=1) * _silu(hg_ref[...].astype(F32))).astype(BF16)
    out = x_ref[...] + _dot(fy, w1_ref[...]) + _dot(hy, w2_ref[...])
    if final:
        inv = lax.rsqrt(jnp.mean(out * out, axis=-1, keepdims=True) + NORM_EPS)
        out = (out * inv) * fgain_ref[...]
    y_ref[...] = out


def _merge(x, oT, fg, ho, hg, gn, w1, w2, fgain, nhg, final):
    b, t, d = x.shape
    fw = fg.shape[2]
    tm = min(PROJ_TM, t)
    seq = lambda bi, s: (bi, s, 0)
    const = lambda bi, s: (0, 0)
    blk = pl.BlockSpec((None, tm, fw), seq)
    return pl.pallas_call(
        functools.partial(_merge_body, final, nhg),
        grid=(b, t // tm),
        in_specs=[pl.BlockSpec((None, tm, d), seq),
                  pl.BlockSpec((None, fw, tm), lambda bi, s: (bi, 0, s)),
                  blk, blk, blk, pl.BlockSpec(gn.shape, const),
                  pl.BlockSpec(w1.shape, const), pl.BlockSpec(w2.shape, const),
                  pl.BlockSpec(fgain.shape, const)],
        out_specs=pl.BlockSpec((None, tm, d), seq),
        out_shape=jax.ShapeDtypeStruct((b, t, d), F32),
        compiler_params=_cparams(2),
        name="merge",
    )(x, oT, fg, ho, hg, gn, w1, w2, fgain)


def _layer(x, t_valid, cache, s0, wts, layer, final):
    b, t, d = x.shape
    nh, hd, nhg = wts["nh"], wts["hd"], wts["nhg"]
    q, k, v, lf, fg, hq, hlf, hk, hi, hg = _proj(
        x.reshape(b * t, d), wts["g"][layer], wts["wm"][layer], wts["wf"][layer],
        wts["bf"][layer], wts["hl"], layer, float(hd) ** -0.5)
    r3 = lambda a: a.reshape(b, t, a.shape[-1])
    q, k, v, lf, fg, hq, hlf, hk, hi, hg = map(r3, (q, k, v, lf, fg, hq, hlf, hk, hi, hg))
    qT, kp, vT, stats = _prep(q, (k, v, lf), cache, nh, hd)
    stats = jnp.transpose(stats, (0, 3, 2, 1))[:, :nh]
    oT = _attn(qT, kp, vT, stats)
    ho, s_new = _hgrn(hq, hlf, hk, hi, s0, t_valid)
    if t_valid < t:
        ho = jnp.pad(ho, ((0, 0), (0, t - t_valid), (0, 0)))
    y = _merge(x, oT, fg, ho, hg, wts["gn"][layer], wts["w1"][layer], wts["w2"][layer],
               wts["fgain"], nhg, final)
    kv_shape = (b, t_valid, nh, hd)
    return (y, k[:, :t_valid].reshape(kv_shape), v[:, :t_valid].reshape(kv_shape),
            lf[:, :t_valid, :nh], s_new)


def kernel(x_prompt, x_sample, cache_k, cache_v, cache_logf, state_hgrn,
           norm_g, w_in, fox_b_f, hg_lower, hg_norm_g, w_out, final_g):
    depth, d = norm_g.shape
    nh = fox_b_f.shape[1]
    fw = w_out.shape[1] // 2
    hd = fw // nh
    nhg = state_hgrn.shape[2]
    o = 3 * fw
    wm = jnp.concatenate([w_in[:, :, :o], w_in[:, :, o + nh:]], axis=2).astype(BF16)
    wf = jnp.pad(w_in[:, :, o:o + nh], ((0, 0), (0, 0), (0, LANES - nh))).astype(BF16)
    wts = dict(
        nh=nh, hd=hd, nhg=nhg, wm=wm, wf=wf,
        g=norm_g.reshape(depth, 1, d),
        bf=jnp.pad(fox_b_f, ((0, 0), (0, LANES - nh))).reshape(depth, 1, LANES),
        hl=hg_lower, gn=hg_norm_g.reshape(depth, 1, -1),
        w1=w_out[:, :fw, :].astype(BF16), w2=w_out[:, fw:, :].astype(BF16),
        fgain=final_g.reshape(1, d))

    bp, sp, _ = x_prompt.shape
    assert sp % (ATT_T * ATT_QBLOCKS) == 0 and sp % PROJ_TM == 0
    xp = x_prompt
    outs_p = []
    zero_state = jnp.zeros((bp,) + state_hgrn.shape[2:], F32)
    for l in range(depth):
        xp, k, v, lf, s_new = _layer(xp, sp, None, zero_state, wts, l, l == depth - 1)
        outs_p.append((k, v, lf, s_new))

    bs, ts_, _ = x_sample.shape
    past = cache_k.shape[2]
    assert past % ATT_T == 0 and ts_ % HG_CHUNK == 0
    t_pad = -(-ts_ // ATT_T) * ATT_T
    xs = jnp.pad(x_sample, ((0, 0), (0, t_pad - ts_), (0, 0)))
    outs_s = []
    for l in range(depth):
        cache = (cache_k[l].reshape(bs, past, fw), cache_v[l].reshape(bs, past, fw),
                 jnp.pad(cache_logf[l], ((0, 0), (0, 0), (0, LANES - nh))))
        xs, k, v, lf, s_new = _layer(xs, ts_, cache, state_hgrn[l], wts, l, l == depth - 1)
        outs_s.append((k, v, lf, s_new))

    stack = lambda outs, i: jnp.stack([o_[i] for o_ in outs])
    return (xp, xs[:, :ts_],
            stack(outs_p, 0), stack(outs_p, 1), stack(outs_p, 2), stack(outs_p, 3),
            stack(outs_s, 0), stack(outs_s, 1), stack(outs_s, 2), stack(outs_s, 3))
```

```python
import functools

import jax
import jax.numpy as jnp
import numpy as np
from jax import lax
from jax.experimental import pallas as pl
from jax.experimental.pallas import tpu as pltpu

F32 = jnp.float32
BF16 = jnp.bfloat16

NORM_EPS = 1e-6
HG_CHUNK = 64
HG_SAFE_LOG_RANGE = 60.0
LANES = 128
SUBLANES = 8
NEG_BIG = -1e30
ATT_SKIP_LOG = 110.0
ATT_NORM_SLACK = 1.02
VMEM_LIMIT = 56 * 1024 * 1024

PROJ_TM = 512
ATT_T = 256
ATT_QBLOCKS = 4
HG_TT = 256

ST_CMAX, ST_CMIN, ST_KNORM, ST_QNORM = 0, 1, 2, 3


def _cparams(n_axes):
    return pltpu.CompilerParams(dimension_semantics=("arbitrary",) * n_axes,
                                vmem_limit_bytes=VMEM_LIMIT)


def _log_sigmoid(x):
    return jnp.minimum(x, 0.0) - jnp.log1p(jnp.exp(-jnp.abs(x)))


def _silu(x):
    return x / (1.0 + jnp.exp(-x))


def _split3(a):
    hi = a.astype(BF16)
    r = a - hi.astype(F32)
    mid = r.astype(BF16)
    lo = (r - mid.astype(F32)).astype(BF16)
    return hi, mid, lo


def _dot(a, b):
    return jnp.dot(a, b, preferred_element_type=F32)


def _dot_nt(a, b):
    return lax.dot_general(a, b, (((1,), (1,)), ((), ())), preferred_element_type=F32)


def _proj_body(layer, fw, nh, scale, x_ref, g_ref, wm_ref, wf_ref, bf_ref, hl_ref, *rest):
    (q_ref, kT_ref, vT_ref, lfT_ref, fg_ref, hq_ref, hlf_ref, hk_ref, hi_ref, hg_ref) = rest[-10:]
    x = x_ref[...]
    inv = lax.rsqrt(jnp.mean(x * x, axis=-1, keepdims=True) + NORM_EPS)
    hb = ((x * inv) * g_ref[...]).astype(BF16)

    def col(i):
        return _dot(hb, wm_ref[:, i * fw:(i + 1) * fw])

    q_ref[...] = (col(0) * scale).astype(BF16)
    kT_ref[...] = col(1).T
    vT_ref[...] = col(2).T
    lf = _log_sigmoid(_dot(hb, wf_ref[...]) + bf_ref[...])
    lfT_ref[...] = lf.T[:nh, :]
    fg_ref[...] = col(3).astype(BF16)
    hq_ref[...] = _silu(col(4)).astype(BF16)

    rows = [hl_ref[i:i + 1, :] for i in range(hl_ref.shape[0])]
    mx = functools.reduce(jnp.maximum, rows)
    es = [jnp.exp(r - mx) for r in rows]
    tot = functools.reduce(lambda a, b: a + b, es)
    lb = jnp.zeros_like(mx)
    for i in range(1, layer + 1):
        lb = lb + es[i] / tot

    zf = col(5)
    a = jnp.log(lb)
    b = jnp.log1p(-lb) + _log_sigmoid(zf)
    hlf_ref[...] = jnp.maximum(a, b) + jnp.log1p(jnp.exp(-jnp.abs(a - b)))
    hk_ref[...] = ((1.0 - lb) / (1.0 + jnp.exp(zf))).astype(BF16)
    hi_ref[...] = col(6).astype(BF16)
    hg_ref[...] = col(7).astype(BF16)


def _proj(x, wts, layer, prev):
    b, t, d = x.shape
    nh, hd = wts["nh"], wts["hd"]
    depth = wts["hl"].shape[0]
    fw = nh * hd
    tm = min(PROJ_TM, t)
    seq = lambda bi, s: (bi, s, 0)
    featT = lambda bi, s: (layer, bi, 0, s)
    c2 = lambda bi, s: (0, 0)
    g, wm, wf, bfp = wts["g"][layer], wts["wm"][layer], wts["wf"][layer], wts["bf"][layer]
    in_specs = [pl.BlockSpec((None, tm, d), seq), pl.BlockSpec(g.shape, c2),
                pl.BlockSpec(wm.shape, c2), pl.BlockSpec(wf.shape, c2),
                pl.BlockSpec(bfp.shape, c2), pl.BlockSpec(wts["hl"].shape, c2)]
    args = [x, g, wm, wf, bfp, wts["hl"]]
    aliases = {}
    if prev is not None:
        for i, a in enumerate(prev):
            in_specs.append(pl.BlockSpec(memory_space=pl.ANY))
            aliases[len(args)] = 1 + i
            args.append(a)
    tok = pl.BlockSpec((None, tm, fw), seq)
    b16o = jax.ShapeDtypeStruct((b, t, fw), BF16)
    return pl.pallas_call(
        functools.partial(_proj_body, layer, fw, nh, float(hd) ** -0.5),
        grid=(b, t // tm),
        in_specs=in_specs,
        out_specs=[tok, pl.BlockSpec((None, None, fw, tm), featT), pl.BlockSpec((None, None, fw, tm), featT),
                   pl.BlockSpec((None, None, nh, tm), featT), tok, tok, tok, tok, tok, tok],
        out_shape=[b16o, jax.ShapeDtypeStruct((depth, b, fw, t), F32),
                   jax.ShapeDtypeStruct((depth, b, fw, t), F32),
                   jax.ShapeDtypeStruct((depth, b, nh, t), F32),
                   b16o, b16o, jax.ShapeDtypeStruct((b, t, fw), F32), b16o, b16o, b16o],
        input_output_aliases=aliases,
        compiler_params=_cparams(2),
        name="proj",
    )(*args)


def _prep_body(nh, hd, n_past, *refs):
    if n_past:
        (q_ref, kc_ref, vc_ref, lfc_ref, kn_ref, vn_ref, lfn_ref, tri_ref, hsel_ref, plc_ref,
         qT_ref, kp_ref, vT_ref, st_ref, carry_ref) = refs
        is_new = pl.program_id(1) >= n_past
        kt = jnp.where(is_new, kn_ref[...], kc_ref[...])
        vt = jnp.where(is_new, vn_ref[...], vc_ref[...])
        lft = jnp.where(is_new, lfn_ref[...], lfc_ref[...])
    else:
        (q_ref, kn_ref, vn_ref, lfn_ref, tri_ref, hsel_ref, plc_ref,
         qT_ref, kp_ref, vT_ref, st_ref, carry_ref) = refs
        kt, vt, lft = kn_ref[...], vn_ref[...], lfn_ref[...]

    @pl.when(pl.program_id(1) == 0)
    def _():
        carry_ref[...] = jnp.zeros_like(carry_ref)

    ts = lft.shape[1]
    pw = nh * LANES
    per = LANES // hd
    tri = tri_ref[...]
    ct = carry_ref[:, 0:1]
    for part in _split3(lft):
        ct = ct + _dot(part, tri)
    carry_ref[...] = jnp.broadcast_to(ct[:, ts - 1:ts], carry_ref.shape)
    c = jnp.concatenate([ct, jnp.zeros((LANES - nh, ts), F32)], axis=0).T

    hi, mid, lo = (p.astype(F32) for p in _split3(c))
    lane = lax.broadcasted_iota(jnp.int32, (1, LANES), 1)
    cpack = jnp.where(lane < nh, hi, jnp.where(lane < 2 * nh, pltpu.roll(mid, nh, axis=1),
                                               pltpu.roll(lo, 2 * nh, axis=1)))
    aug = _dot(cpack.astype(BF16), plc_ref[...])

    def spread(xx):
        pieces = []
        for p in range(nh // per):
            src = xx[:, p * LANES:(p + 1) * LANES]
            pieces.append(src)
            for r in range(1, per):
                pieces.append(pltpu.roll(src, LANES - r * hd, axis=1))
        return jnp.concatenate(pieces, axis=1)

    lane_in = lax.broadcasted_iota(jnp.int32, (1, pw), 1) & (LANES - 1)
    m_data = lane_in < hd
    m_pos = jnp.logical_and(lane_in >= hd, lane_in < hd + 3)
    m_neg = jnp.logical_and(lane_in >= hd + 3, lane_in < hd + 6)
    q = q_ref[...]
    kb = kt.T.astype(BF16)
    qp = jnp.where(m_data, spread(q.astype(F32)), jnp.where(m_pos, aug, jnp.where(m_neg, 1.0, 0.0)))
    kp = jnp.where(m_data, spread(kb.astype(F32)), jnp.where(m_neg, aug, jnp.where(m_pos, 1.0, 0.0)))
    kp_ref[...] = kp.astype(BF16)
    for h in range(nh):
        qT_ref[h] = qp[:, LANES * h:LANES * (h + 1)].T.astype(BF16)
        vT_ref[h] = vt[hd * h:hd * (h + 1), :].astype(BF16)

    hsel = hsel_ref[...]
    kf = kb.astype(F32)
    qf = q.astype(F32)
    kn = jnp.sqrt(jnp.max(_dot((kf * kf).astype(BF16), hsel), axis=0, keepdims=True))
    qn = jnp.sqrt(jnp.max(_dot((qf * qf).astype(BF16), hsel), axis=0, keepdims=True))
    st_ref[...] = jnp.concatenate(
        [jnp.max(c, axis=0, keepdims=True), jnp.min(c, axis=0, keepdims=True), kn, qn,
         jnp.zeros((4, LANES), F32)], axis=0)


def _prep_constants(nh, hd, ts):
    fw = nh * hd
    pw = nh * LANES
    tri = np.triu(np.ones((ts, ts), np.float32))
    hsel = np.zeros((fw, LANES), np.float32)
    plc = np.zeros((LANES, pw), np.float32)
    for h in range(nh):
        hsel[h * hd:(h + 1) * hd, h] = 1.0
        for i in range(3):
            plc[i * nh + h, h * LANES + hd + i] = 1.0
            plc[i * nh + h, h * LANES + hd + 3 + i] = -1.0
    return jnp.asarray(tri, BF16), jnp.asarray(hsel, BF16), jnp.asarray(plc, BF16)


def _prep(q, new, cache, layer, nh, hd):
    b, tq, fw = q.shape
    ts = ATT_T
    n_past = 0 if cache is None else cache[0].shape[3] // ts
    nblk = n_past + tq // ts
    consts = _prep_constants(nh, hd, ts)
    pw = nh * LANES
    new_blk = lambda s: jnp.maximum(s - n_past, 0)
    featT_new = lambda bi, s: (layer, bi, 0, new_blk(s))
    featT_old = lambda bi, s: (layer, bi, 0, jnp.minimum(s, n_past - 1))

    def feat_specs(idx):
        return [pl.BlockSpec((None, None, fw, ts), idx), pl.BlockSpec((None, None, fw, ts), idx),
                pl.BlockSpec((None, None, nh, ts), idx)]

    in_specs = [pl.BlockSpec((None, ts, fw), lambda bi, s: (bi, new_blk(s), 0))]
    args = [q]
    if n_past:
        in_specs += feat_specs(featT_old)
        args += list(cache)
    in_specs += feat_specs(featT_new)
    args += list(new)
    for cst in consts:
        in_specs.append(pl.BlockSpec(cst.shape, lambda bi, s: (0, 0)))
    args += list(consts)
    return pl.pallas_call(
        functools.partial(_prep_body, nh, hd, n_past),
        grid=(b, nblk),
        in_specs=in_specs,
        out_specs=[pl.BlockSpec((None, nh, LANES, ts), lambda bi, s: (bi, 0, 0, new_blk(s))),
                   pl.BlockSpec((None, ts, pw), lambda bi, s: (bi, s, 0)),
                   pl.BlockSpec((None, nh, None, hd, ts), lambda bi, s: (bi, 0, s, 0, 0)),
                   pl.BlockSpec((None, None, SUBLANES, LANES), lambda bi, s: (bi, s, 0, 0))],
        out_shape=[jax.ShapeDtypeStruct((b, nh, LANES, tq), BF16),
                   jax.ShapeDtypeStruct((b, nblk * ts, pw), BF16),
                   jax.ShapeDtypeStruct((b, nh, nblk, hd, ts), BF16),
                   jax.ShapeDtypeStruct((b, nblk, SUBLANES, LANES), F32)],
        scratch_shapes=[pltpu.VMEM((nh, LANES), F32)],
        compiler_params=_cparams(2),
        name="prep",
    )(*args)


def _attn_body(q_off, hd, nqb, qT_ref, kp_ref, vT_ref, st_ref, o_ref):
    t = ATT_T
    nblk = st_ref.shape[1]
    st = st_ref[...]
    blk_id = lax.broadcasted_iota(jnp.int32, (1, nblk), 1)
    krow_minus_qcol = (lax.broadcasted_iota(jnp.int32, (2 * t, t), 0)
                       - lax.broadcasted_iota(jnp.int32, (2 * t, t), 1))

    def online(j, carry, qT):
        m, l, acc = carry
        kb = kp_ref[pl.ds(pl.multiple_of(j * t, t), t), :]
        s = _dot(kb, qT)
        m_new = jnp.maximum(m, jnp.max(s, axis=0, keepdims=True))
        alpha = jnp.exp(m - m_new)
        p = jnp.exp(s - m_new)
        l = alpha * l + jnp.sum(p, axis=0, keepdims=True)
        acc = alpha * acc + _dot(vT_ref[j], p.astype(BF16))
        return m_new, l, acc

    heads = []
    for u in range(nqb):
        qblk = q_off + pl.program_id(1) * nqb + u
        wstart = jnp.maximum(qblk - 1, 0)
        qT = qT_ref[:, u * t:(u + 1) * t]
        kb = kp_ref[pl.ds(pl.multiple_of(wstart * t, t), 2 * t), :]
        s = _dot(kb, qT)
        heads.append((qblk, wstart, qT, s))

    chains = []
    for qblk, wstart, qT, s in heads:
        s = jnp.where(krow_minus_qcol <= (qblk - wstart) * t, s, NEG_BIG)
        m = jnp.max(s, axis=0, keepdims=True)
        p = jnp.exp(s - m)
        l = jnp.sum(p, axis=0, keepdims=True)
        pb = p.astype(BF16)
        acc = _dot(vT_ref[wstart], pb[:t, :]) + _dot(vT_ref[wstart + 1], pb[t:, :])

        sel = blk_id == qblk
        pick = lambda r: jnp.sum(jnp.where(sel, st[r:r + 1, :], 0.0), axis=1, keepdims=True)
        bound = (ATT_NORM_SLACK * pick(ST_QNORM) * (st[ST_KNORM:ST_KNORM + 1, :] + pick(ST_KNORM))
                 + (pick(ST_CMAX) - st[ST_CMIN:ST_CMIN + 1, :]))
        live = jnp.logical_and(bound >= -ATT_SKIP_LOG, blk_id < wstart)
        first = jnp.min(jnp.where(live, blk_id, wstart))
        chains.append((qT, wstart, first, (m, l, acc)))

    for u, (qT, wstart, first, carry) in enumerate(chains):
        _, l, acc = lax.fori_loop(first, wstart, lambda j, c, qT=qT: online(j, c, qT), carry)
        o_ref[:, u * t:(u + 1) * t] = acc / l


def _attn(qT, kp, vT, stats):
    b, nh, _, tq_all = qT.shape
    tk_all = kp.shape[1]
    hd = vT.shape[3]
    t = ATT_T
    nblk = tk_all // t
    q_off = (tk_all - tq_all) // t
    nqb = ATT_QBLOCKS if (tq_all // t) % ATT_QBLOCKS == 0 else 1
    tq = nqb * t
    return pl.pallas_call(
        functools.partial(_attn_body, q_off, hd, nqb),
        grid=(b * nh, tq_all // tq),
        in_specs=[pl.BlockSpec((None, None, LANES, tq), lambda g, i: (g // nh, g % nh, 0, i)),
                  pl.BlockSpec((None, tk_all, LANES), lambda g, i: (g // nh, 0, g % nh)),
                  pl.BlockSpec((None, None, nblk, hd, t), lambda g, i: (g // nh, g % nh, 0, 0, 0)),
                  pl.BlockSpec((None, None, SUBLANES, nblk), lambda g, i: (g // nh, g % nh, 0, 0))],
        out_specs=pl.BlockSpec((None, hd, tq), lambda g, i: (g // nh, g % nh, i)),
        out_shape=jax.ShapeDtypeStruct((b, nh * hd, tq_all), F32),
        compiler_params=_cparams(2),
        name="attn",
    )(qT, kp, vT, stats)


def _hgrn_body(nhg, q_ref, lf_ref, k_ref, v_ref, s0_ref, tri_ref, o_ref, sfin_ref,
               st_ref, b_ref, qf_ref, kf_ref, oi_ref, qb_ref, u_ref, dec_ref):
    tt, w = lf_ref.shape
    hd = w // nhg
    cl = HG_CHUNK
    nchunk = tt // cl

    @pl.when(pl.program_id(1) == 0)
    def _():
        for h in range(nhg):
            st_ref[h] = s0_ref[h].T

    tri = tri_ref[...]
    b = jnp.zeros((tt, w), F32)
    for part in _split3(lf_ref[...]):
        b = b + _dot(tri, part)
    qf = q_ref[...].astype(F32)
    kf = k_ref[...].astype(F32)
    for h in range(nhg):
        hs = slice(h * hd, (h + 1) * hd)
        b_ref[h], qf_ref[h], kf_ref[h] = b[:, hs], qf[:, hs], kf[:, hs]

    rid = lax.broadcasted_iota(jnp.int32, (cl, cl), 0)
    cid = lax.broadcasted_iota(jnp.int32, (cl, cl), 1)
    zpad = jnp.zeros((LANES - cl, hd), F32)
    worst = jnp.zeros((1, hd), F32)
    units = [(h, c) for h in range(nhg) for c in range(nchunk)]
    scores = {}
    for h, c in units:
        hs, sl = slice(h * hd, (h + 1) * hd), slice(c * cl, (c + 1) * cl)
        bc, qc, kc = b[sl, hs], qf[sl, hs], kf[sl, hs]
        vc = v_ref[sl, hs]
        r = bc[cl // 2 - 1:cl // 2, :]
        last = bc[cl - 1:cl, :]
        worst = jnp.maximum(worst, jnp.maximum(-r, r - last))
        qe = (qc * jnp.exp(bc - r)).astype(BF16)
        ke = (kc * jnp.exp(r - bc)).astype(BF16)
        scores[h, c] = _dot_nt(qe, ke)
        qb_ref[h, sl, :] = (qc * jnp.exp(bc)).astype(BF16)
        ke2 = jnp.concatenate([kc * jnp.exp(last - bc), zpad], axis=0).astype(BF16)
        vt = jnp.concatenate([vc.astype(F32), zpad], axis=0).T.astype(BF16)
        u_ref[h, c] = _dot(vt, ke2)
        dec_ref[h * nchunk + c:h * nchunk + c + 1, :] = jnp.exp(last)
    for h, c in units:
        hs, sl = slice(h * hd, (h + 1) * hd), slice(c * cl, (c + 1) * cl)
        a = jnp.where(rid >= cid, scores[h, c], 0.0)
        oi_ref[h, sl, :] = _dot(a.astype(BF16), v_ref[sl, hs])

    states = [st_ref[h] for h in range(nhg)]
    for c in range(nchunk):
        sl = slice(c * cl, (c + 1) * cl)
        for h in range(nhg):
            hs = slice(h * hd, (h + 1) * hd)
            o_ref[sl, hs] = _dot_nt(qb_ref[h, sl, :], states[h].astype(BF16))
            i = h * nchunk + c
            states[h] = states[h] * dec_ref[i:i + 1, :] + u_ref[h, c]
    for h in range(nhg):
        st_ref[h] = states[h]

    def row(t, _):
        base = pl.multiple_of((t // cl) * cl, cl)
        spos = base + lax.broadcasted_iota(jnp.int32, (cl, 1), 0)
        for h in range(nhg):
            hs = slice(h * hd, (h + 1) * hd)
            bt = b_ref[h, pl.ds(t, 1), :]
            qt = qf_ref[h, pl.ds(t, 1), :]
            e = jnp.exp(jnp.minimum(bt - b_ref[h, pl.ds(base, cl), :], 0.0))
            wgt = jnp.where(spos <= t, qt * kf_ref[h, pl.ds(base, cl), :] * e, 0.0)
            a = jnp.sum(wgt, axis=1, keepdims=True)
            vch = v_ref[pl.ds(base, cl), hs].astype(F32)
            oi_ref[h, pl.ds(t, 1), :] = jnp.sum(a * vch, axis=0, keepdims=True)
        return 0

    unsafe_rows = jnp.where(jnp.max(worst) <= HG_SAFE_LOG_RANGE, 0, tt)
    lax.fori_loop(0, unsafe_rows, row, 0)
    for h in range(nhg):
        hs = slice(h * hd, (h + 1) * hd)
        o_ref[:, hs] = o_ref[:, hs] + oi_ref[h]

    @pl.when(pl.program_id(1) == pl.num_programs(1) - 1)
    def _():
        for h in range(nhg):
            sfin_ref[h] = st_ref[h].T


def _hgrn(hq, hlf, hk, hi, s0, t_valid):
    b, _, w = hq.shape
    nhg, hd = s0.shape[1], s0.shape[2]
    tt = min(HG_TT, t_valid)
    cl = HG_CHUNK
    nchunk = tt // cl
    tri = jnp.asarray(np.kron(np.eye(nchunk, dtype=np.float32),
                              np.tril(np.ones((cl, cl), np.float32))), BF16)
    seq = lambda bi, s: (bi, s, 0)
    sidx = lambda bi, s: (bi, 0, 0, 0)
    blk = pl.BlockSpec((None, tt, w), seq)
    sblk = pl.BlockSpec((None, nhg, hd, hd), sidx)
    head_rows = pltpu.VMEM((nhg, tt, hd), F32)
    return pl.pallas_call(
        functools.partial(_hgrn_body, nhg),
        grid=(b, t_valid // tt),
        in_specs=[blk, blk, blk, blk, sblk, pl.BlockSpec(tri.shape, lambda bi, s: (0, 0))],
        out_specs=[blk, sblk],
        out_shape=[jax.ShapeDtypeStruct((b, t_valid, w), F32),
                   jax.ShapeDtypeStruct((b, nhg, hd, hd), F32)],
        scratch_shapes=[pltpu.VMEM((nhg, hd, hd), F32), head_rows, head_rows, head_rows, head_rows,
                        pltpu.VMEM((nhg, tt, hd), BF16), pltpu.VMEM((nhg, nchunk, hd, hd), F32),
                        pltpu.VMEM((nhg * nchunk, hd), F32)],
        compiler_params=_cparams(2),
        name="hgrn",
    )(hq, hlf, hk, hi, s0, tri)


def _merge_body(final, nhg, x_ref, oT_ref, fg_ref, ho_ref, hg_ref, gn_ref, w1_ref, w2_ref, fgain_ref,
                y_ref):
    fy = (oT_ref[...].T * _silu(fg_ref[...].astype(F32))).astype(BF16)
    ho = ho_ref[...]
    hw = ho.shape[1] // nhg
    normed = []
    for h in range(nhg):
        oh = ho[:, h * hw:(h + 1) * hw]
        inv = lax.rsqrt(jnp.mean(oh * oh, axis=-1, keepdims=True) + NORM_EPS)
        normed.append(oh * inv * gn_ref[:, h * hw:(h + 1) * hw])
    hy = (jnp.concatenate(normed, axis=1) * _silu(hg_ref[...].astype(F32))).astype(BF16)
    out = x_ref[...] + _dot(fy, w1_ref[...]) + _dot(hy, w2_ref[...])
    if final:
        inv = lax.rsqrt(jnp.mean(out * out, axis=-1, keepdims=True) + NORM_EPS)
        out = (out * inv) * fgain_ref[...]
    y_ref[...] = out


def _merge(x, oT, fg, ho, hg, gn, w1, w2, fgain, nhg, final):
    b, t, d = x.shape
    fw = fg.shape[2]
    tm = min(PROJ_TM, t)
    seq = lambda bi, s: (bi, s, 0)
    const = lambda bi, s: (0, 0)
    blk = pl.BlockSpec((None, tm, fw), seq)
    return pl.pallas_call(
        functools.partial(_merge_body, final, nhg),
        grid=(b, t // tm),
        in_specs=[pl.BlockSpec((None, tm, d), seq),
                  pl.BlockSpec((None, fw, tm), lambda bi, s: (bi, 0, s)),
                  blk, blk, blk, pl.BlockSpec(gn.shape, const),
                  pl.BlockSpec(w1.shape, const), pl.BlockSpec(w2.shape, const),
                  pl.BlockSpec(fgain.shape, const)],
        out_specs=pl.BlockSpec((None, tm, d), seq),
        out_shape=jax.ShapeDtypeStruct((b, t, d), F32),
        compiler_params=_cparams(2),
        name="merge",
    )(x, oT, fg, ho, hg, gn, w1, w2, fgain)


def _path(x, t_valid, cache, states0, wts):
    nh, hd, nhg = wts["nh"], wts["hd"], wts["nhg"]
    depth = wts["hl"].shape[0]
    t = x.shape[1]
    feats = None
    s_fin = []
    for layer in range(depth):
        q, kT, vT, lfT, fg, hq, hlf, hk, hi, hg = _proj(x, wts, layer, feats)
        feats = (kT, vT, lfT)
        qT, kp, vTb, stats = _prep(q, feats, cache, layer, nh, hd)
        stats = jnp.transpose(stats, (0, 3, 2, 1))[:, :nh]
        oT = _attn(qT, kp, vTb, stats)
        ho, s_new = _hgrn(hq, hlf, hk, hi, states0[layer], t_valid)
        if t_valid < t:
            ho = jnp.pad(ho, ((0, 0), (0, t - t_valid), (0, 0)))
        x = _merge(x, oT, fg, ho, hg, wts["gn"][layer], wts["w1"][layer], wts["w2"][layer],
                   wts["fgain"], nhg, layer == depth - 1)
        s_fin.append(s_new)
    return x, feats, jnp.stack(s_fin)


def _caches_out(feats, t_valid, nh, hd):
    kT, vT, lfT = (a[..., :t_valid] for a in feats)
    depth, b = kT.shape[:2]
    to_tok = lambda a: jnp.transpose(a.reshape(depth, b, nh, hd, t_valid), (0, 1, 4, 2, 3))
    return to_tok(kT), to_tok(vT), jnp.transpose(lfT, (0, 1, 3, 2))


def kernel(x_prompt, x_sample, cache_k, cache_v, cache_logf, state_hgrn,
           norm_g, w_in, fox_b_f, hg_lower, hg_norm_g, w_out, final_g):
    depth, d = norm_g.shape
    nh = fox_b_f.shape[1]
    fw = w_out.shape[1] // 2
    hd = fw // nh
    nhg = state_hgrn.shape[2]
    assert LANES % hd == 0 and nh % (LANES // hd) == 0 and 3 * nh <= LANES
    o = 3 * fw
    wm = jnp.concatenate([w_in[:, :, :o], w_in[:, :, o + nh:]], axis=2).astype(BF16)
    wf = jnp.pad(w_in[:, :, o:o + nh], ((0, 0), (0, 0), (0, LANES - nh))).astype(BF16)
    wts = dict(
        nh=nh, hd=hd, nhg=nhg, wm=wm, wf=wf,
        g=norm_g.reshape(depth, 1, d),
        bf=jnp.pad(fox_b_f, ((0, 0), (0, LANES - nh))).reshape(depth, 1, LANES),
        hl=hg_lower, gn=hg_norm_g.reshape(depth, 1, -1),
        w1=w_out[:, :fw, :].astype(BF16), w2=w_out[:, fw:, :].astype(BF16),
        fgain=final_g.reshape(1, d))

    bp, sp, _ = x_prompt.shape
    assert sp % (ATT_T * ATT_QBLOCKS) == 0 and sp % PROJ_TM == 0
    zero_states = jnp.zeros((depth, bp) + state_hgrn.shape[2:], F32)
    y_p, feats_p, s_p = _path(x_prompt, sp, None, zero_states, wts)

    bs, ts_, _ = x_sample.shape
    past = cache_k.shape[2]
    assert past % ATT_T == 0 and ts_ % HG_CHUNK == 0
    t_pad = -(-ts_ // ATT_T) * ATT_T
    xs = jnp.pad(x_sample, ((0, 0), (0, t_pad - ts_), (0, 0)))
    to_featT = lambda a: jnp.transpose(a, (0, 1, 3, 4, 2)).reshape(depth, bs, fw, past)
    cache = (to_featT(cache_k), to_featT(cache_v), jnp.transpose(cache_logf, (0, 1, 3, 2)))
    y_s, feats_s, s_s = _path(xs, ts_, cache, state_hgrn, wts)

    k_p, v_p, lf_p = _caches_out(feats_p, sp, nh, hd)
    k_s, v_s, lf_s = _caches_out(feats_s, ts_, nh, hd)
    return (y_p, y_s[:, :ts_], k_p, v_p, lf_p, s_p, k_s, v_s, lf_s, s_s)
```

```python
import functools

import jax
import jax.numpy as jnp
import numpy as np
from jax import lax
from jax.experimental import pallas as pl
from jax.experimental.pallas import tpu as pltpu

F32 = jnp.float32
BF16 = jnp.bfloat16

NORM_EPS = 1e-6
HG_CHUNK = 64
HG_SAFE_LOG_RANGE = 60.0
LANES = 128
SUBLANES = 8
NEG_BIG = -1e30
ATT_SKIP_LOG = 110.0
ATT_NORM_SLACK = 1.02
LOG2_E = 1.4426950408889634
VMEM_LIMIT = 56 * 1024 * 1024

PROJ_TM = 512
ATT_T = 256
ATT_QBLOCKS = 8
HG_TT = 256

ST_CMAX, ST_CMIN, ST_KNORM, ST_QNORM = 0, 1, 2, 3


def _cparams(n_axes):
    return pltpu.CompilerParams(dimension_semantics=("arbitrary",) * n_axes,
                                vmem_limit_bytes=VMEM_LIMIT)


def _log_sigmoid(x):
    return jnp.minimum(x, 0.0) - jnp.log1p(jnp.exp(-jnp.abs(x)))


def _silu(x):
    return x / (1.0 + jnp.exp(-x))


def _split3(a):
    hi = a.astype(BF16)
    r = a - hi.astype(F32)
    mid = r.astype(BF16)
    lo = (r - mid.astype(F32)).astype(BF16)
    return hi, mid, lo


def _dot(a, b):
    return jnp.dot(a, b, preferred_element_type=F32)


def _dot_nt(a, b):
    return lax.dot_general(a, b, (((1,), (1,)), ((), ())), preferred_element_type=F32)


def _proj_body(layer, nh, hd, fused, x_ref, g_ref, wm_ref, wf_ref, bf_ref, hl_ref, *rest):
    if fused:
        tri_ref, hsel_ref, plc_ref = rest[:3]
        (q_ref, kT_ref, vT_ref, lfT_ref, fg_ref, hq_ref, hlf_ref, hk_ref, hi_ref, hg_ref,
         kp_ref, vTb_ref, st_ref, carry_ref) = rest[-14:]
    else:
        (q_ref, kT_ref, vT_ref, lfT_ref, fg_ref, hq_ref, hlf_ref, hk_ref, hi_ref, hg_ref) = rest[-10:]
    fw = nh * hd
    scale = float(hd) ** -0.5 * LOG2_E
    x = x_ref[...]
    inv = lax.rsqrt(jnp.mean(x * x, axis=-1, keepdims=True) + NORM_EPS)
    hb = ((x * inv) * g_ref[...]).astype(BF16)

    def col(i):
        return _dot(hb, wm_ref[:, i * fw:(i + 1) * fw])

    lf = _log_sigmoid(_dot(hb, wf_ref[...]) + bf_ref[...])
    lft = lf.T[:nh, :]
    lfT_ref[...] = lft
    q = (col(0) * scale).astype(BF16)
    k = col(1)
    vt = col(2).T
    kT_ref[...] = k.T
    vT_ref[...] = vt
    if fused:
        @pl.when(pl.program_id(1) == 0)
        def _():
            carry_ref[...] = jnp.zeros_like(carry_ref)

        vTb_ref[...] = vt.astype(BF16)
        ct = _cumulative(lft, tri_ref[...], carry_ref)
    else:
        q_ref[...] = q
    fg_ref[...] = col(3).astype(BF16)
    hq_ref[...] = _silu(col(4)).astype(BF16)
    if fused:
        kb = k.astype(BF16)
        for i in range(q.shape[0] // ATT_T):
            sl = slice(i * ATT_T, (i + 1) * ATT_T)
            q_tiles, kp, stats = _operands(q[sl, :], kb[sl, :], ct[:, sl], hsel_ref[...], plc_ref[...],
                                           nh, hd)
            kp_ref[sl, :] = kp
            for h in range(nh):
                q_ref[h, :, sl] = q_tiles[h]
            st_ref[i] = stats

    rows = [hl_ref[i:i + 1, :] for i in range(hl_ref.shape[0])]
    mx = functools.reduce(jnp.maximum, rows)
    es = [jnp.exp(r - mx) for r in rows]
    tot = functools.reduce(lambda a, b: a + b, es)
    lb = jnp.zeros_like(mx)
    for i in range(1, layer + 1):
        lb = lb + es[i] / tot

    zf = col(5)
    a = jnp.log(lb)
    b = jnp.log1p(-lb) + _log_sigmoid(zf)
    hlf_ref[...] = jnp.maximum(a, b) + jnp.log1p(jnp.exp(-jnp.abs(a - b)))
    hk_ref[...] = ((1.0 - lb) / (1.0 + jnp.exp(zf))).astype(BF16)
    hi_ref[...] = col(6).astype(BF16)
    hg_ref[...] = col(7).astype(BF16)


def _proj(x, wts, layer, prev, fused):
    b, t, d = x.shape
    nh, hd = wts["nh"], wts["hd"]
    depth = wts["hl"].shape[0]
    fw = nh * hd
    pw = nh * LANES
    tm = min(PROJ_TM, t)
    seq = lambda bi, s: (bi, s, 0)
    featT = lambda bi, s: (layer, bi, 0, s)
    c2 = lambda bi, s: (0, 0)
    g, wm, wf, bfp = wts["g"][layer], wts["wm"][layer], wts["wf"][layer], wts["bf"][layer]
    args = [x, g, wm, wf, bfp, wts["hl"]]
    in_specs = [pl.BlockSpec((None, tm, d), seq)] + [pl.BlockSpec(a.shape, c2) for a in args[1:]]
    if fused:
        consts = _prep_constants(nh, hd, tm)
        in_specs += [pl.BlockSpec(a.shape, c2) for a in consts]
        args += list(consts)
    aliases = {}
    if prev is not None:
        for i, a in enumerate(prev):
            in_specs.append(pl.BlockSpec(memory_space=pl.ANY))
            aliases[len(args)] = 1 + i
            args.append(a)
    tok = pl.BlockSpec((None, tm, fw), seq)
    b16o = jax.ShapeDtypeStruct((b, t, fw), BF16)
    feat = pl.BlockSpec((None, None, fw, tm), featT)
    out_specs = [tok, feat, feat, pl.BlockSpec((None, None, nh, tm), featT), tok, tok, tok, tok, tok, tok]
    out_shape = [b16o, jax.ShapeDtypeStruct((depth, b, fw, t), F32),
                 jax.ShapeDtypeStruct((depth, b, fw, t), F32),
                 jax.ShapeDtypeStruct((depth, b, nh, t), F32),
                 b16o, b16o, jax.ShapeDtypeStruct((b, t, fw), F32), b16o, b16o, b16o]
    scratch = []
    if fused:
        out_specs[0] = pl.BlockSpec((None, nh, LANES, tm), lambda bi, s: (bi, 0, 0, s))
        out_shape[0] = jax.ShapeDtypeStruct((b, nh, LANES, t), BF16)
        out_specs += [pl.BlockSpec((None, tm, pw), seq), pl.BlockSpec((None, fw, tm), lambda bi, s: (bi, 0, s)),
                      pl.BlockSpec((None, tm // ATT_T, SUBLANES, LANES), lambda bi, s: (bi, s, 0, 0))]
        out_shape += [jax.ShapeDtypeStruct((b, t, pw), BF16), jax.ShapeDtypeStruct((b, fw, t), BF16),
                      jax.ShapeDtypeStruct((b, t // ATT_T, SUBLANES, LANES), F32)]
        scratch = [pltpu.VMEM((nh, LANES), F32)]
    return pl.pallas_call(
        functools.partial(_proj_body, layer, nh, hd, fused),
        grid=(b, t // tm),
        in_specs=in_specs,
        out_specs=out_specs,
        out_shape=out_shape,
        scratch_shapes=scratch,
        input_output_aliases=aliases,
        compiler_params=_cparams(2),
        name="proj",
    )(*args)


def _cumulative(lft, tri, carry_ref):
    n = lft.shape[1]
    ct = carry_ref[:, 0:1]
    for part in _split3(lft):
        ct = ct + _dot(part, tri)
    carry_ref[...] = jnp.broadcast_to(ct[:, n - 1:n], carry_ref.shape)
    return ct


def _operands(q, kb, ct, hsel, plc, nh, hd):
    ts = ct.shape[1]
    pw = nh * LANES
    per = LANES // hd
    c = jnp.concatenate([ct * LOG2_E, jnp.zeros((LANES - nh, ts), F32)], axis=0).T

    hi, mid, lo = (p.astype(F32) for p in _split3(c))
    lane = lax.broadcasted_iota(jnp.int32, (1, LANES), 1)
    cpack = jnp.where(lane < nh, hi, jnp.where(lane < 2 * nh, pltpu.roll(mid, nh, axis=1),
                                               pltpu.roll(lo, 2 * nh, axis=1)))
    aug = _dot(cpack.astype(BF16), plc)

    def spread(xx):
        pieces = []
        for p in range(nh // per):
            src = xx[:, p * LANES:(p + 1) * LANES]
            pieces.append(src)
            for r in range(1, per):
                pieces.append(pltpu.roll(src, LANES - r * hd, axis=1))
        return jnp.concatenate(pieces, axis=1)

    lane_in = lax.broadcasted_iota(jnp.int32, (1, pw), 1) & (LANES - 1)
    m_data = lane_in < hd
    m_pos = jnp.logical_and(lane_in >= hd, lane_in < hd + 3)
    m_neg = jnp.logical_and(lane_in >= hd + 3, lane_in < hd + 6)
    qf = q.astype(F32)
    kf = kb.astype(F32)
    qp = jnp.where(m_data, spread(qf), jnp.where(m_pos, aug, jnp.where(m_neg, 1.0, 0.0)))
    kp = jnp.where(m_data, spread(kf), jnp.where(m_neg, aug, jnp.where(m_pos, 1.0, 0.0)))
    q_tiles = [qp[:, LANES * h:LANES * (h + 1)].T.astype(BF16) for h in range(nh)]

    kn = jnp.sqrt(jnp.max(_dot((kf * kf).astype(BF16), hsel), axis=0, keepdims=True))
    qn = jnp.sqrt(jnp.max(_dot((qf * qf).astype(BF16), hsel), axis=0, keepdims=True))
    stats = jnp.concatenate(
        [jnp.max(c, axis=0, keepdims=True), jnp.min(c, axis=0, keepdims=True), kn, qn,
         jnp.zeros((4, LANES), F32)], axis=0)
    return q_tiles, kp.astype(BF16), stats


def _prep_body(nh, hd, n_past, *refs):
    if n_past:
        (q_ref, kc_ref, vc_ref, lfc_ref, kn_ref, vn_ref, lfn_ref, tri_ref, hsel_ref, plc_ref,
         qT_ref, kp_ref, vT_ref, st_ref, carry_ref) = refs
        is_new = pl.program_id(1) >= n_past
        kt = jnp.where(is_new, kn_ref[...], kc_ref[...])
        vt = jnp.where(is_new, vn_ref[...], vc_ref[...])
        lft = jnp.where(is_new, lfn_ref[...], lfc_ref[...])
    else:
        (q_ref, kn_ref, vn_ref, lfn_ref, tri_ref, hsel_ref, plc_ref,
         qT_ref, kp_ref, vT_ref, st_ref, carry_ref) = refs
        kt, vt, lft = kn_ref[...], vn_ref[...], lfn_ref[...]

    @pl.when(pl.program_id(1) == 0)
    def _():
        carry_ref[...] = jnp.zeros_like(carry_ref)

    ct = _cumulative(lft, tri_ref[...], carry_ref)
    q_tiles, kp, stats = _operands(q_ref[...], kt.T.astype(BF16), ct, hsel_ref[...], plc_ref[...],
                                   nh, hd)
    kp_ref[...] = kp
    for h in range(nh):
        qT_ref[h] = q_tiles[h]
    vT_ref[...] = vt.astype(BF16)
    st_ref[...] = stats


def _prep_constants(nh, hd, ts):
    fw = nh * hd
    pw = nh * LANES
    tri = np.triu(np.ones((ts, ts), np.float32))
    hsel = np.zeros((fw, LANES), np.float32)
    plc = np.zeros((LANES, pw), np.float32)
    for h in range(nh):
        hsel[h * hd:(h + 1) * hd, h] = 1.0
        for i in range(3):
            plc[i * nh + h, h * LANES + hd + i] = 1.0
            plc[i * nh + h, h * LANES + hd + 3 + i] = -1.0
    return jnp.asarray(tri, BF16), jnp.asarray(hsel, BF16), jnp.asarray(plc, BF16)


def _prep(q, new, cache, layer, nh, hd):
    b, tq, fw = q.shape
    ts = ATT_T
    n_past = 0 if cache is None else cache[0].shape[3] // ts
    nblk = n_past + tq // ts
    consts = _prep_constants(nh, hd, ts)
    pw = nh * LANES
    new_blk = lambda s: jnp.maximum(s - n_past, 0)
    featT_new = lambda bi, s: (layer, bi, 0, new_blk(s))
    featT_old = lambda bi, s: (layer, bi, 0, jnp.minimum(s, n_past - 1))

    def feat_specs(idx):
        return [pl.BlockSpec((None, None, fw, ts), idx), pl.BlockSpec((None, None, fw, ts), idx),
                pl.BlockSpec((None, None, nh, ts), idx)]

    in_specs = [pl.BlockSpec((None, ts, fw), lambda bi, s: (bi, new_blk(s), 0))]
    args = [q]
    if n_past:
        in_specs += feat_specs(featT_old)
        args += list(cache)
    in_specs += feat_specs(featT_new)
    args += list(new)
    for cst in consts:
        in_specs.append(pl.BlockSpec(cst.shape, lambda bi, s: (0, 0)))
    args += list(consts)
    return pl.pallas_call(
        functools.partial(_prep_body, nh, hd, n_past),
        grid=(b, nblk),
        in_specs=in_specs,
        out_specs=[pl.BlockSpec((None, nh, LANES, ts), lambda bi, s: (bi, 0, 0, new_blk(s))),
                   pl.BlockSpec((None, ts, pw), lambda bi, s: (bi, s, 0)),
                   pl.BlockSpec((None, fw, ts), lambda bi, s: (bi, 0, s)),
                   pl.BlockSpec((None, None, SUBLANES, LANES), lambda bi, s: (bi, s, 0, 0))],
        out_shape=[jax.ShapeDtypeStruct((b, nh, LANES, tq), BF16),
                   jax.ShapeDtypeStruct((b, nblk * ts, pw), BF16),
                   jax.ShapeDtypeStruct((b, fw, nblk * ts), BF16),
                   jax.ShapeDtypeStruct((b, nblk, SUBLANES, LANES), F32)],
        scratch_shapes=[pltpu.VMEM((nh, LANES), F32)],
        compiler_params=_cparams(2),
        name="prep",
    )(*args)


def _attn_body(q_off, hd, nqb, nhs, qT_ref, kp_ref, vT_ref, st_ref, o_ref):
    t = ATT_T
    nblk = st_ref.shape[2]
    blk_id = lax.broadcasted_iota(jnp.int32, (1, nblk), 1)
    krow_minus_qcol = (lax.broadcasted_iota(jnp.int32, (2 * t, t), 0)
                       - lax.broadcasted_iota(jnp.int32, (2 * t, t), 1))
    rows = lambda j, n: pl.ds(pl.multiple_of(j * t, t), n)

    def online(j, carry, qT, hh):
        m, l, acc = carry
        s = _dot(kp_ref[rows(j, t), hh * LANES:(hh + 1) * LANES], qT)
        m_new = jnp.maximum(m, jnp.max(s, axis=0, keepdims=True))
        alpha = jnp.exp2(m - m_new)
        p = jnp.exp2(s - m_new)
        l = alpha * l + jnp.sum(p, axis=0, keepdims=True)
        acc = alpha * acc + _dot(vT_ref[hh * hd:(hh + 1) * hd, rows(j, t)], p.astype(BF16))
        return m_new, l, acc

    heads = []
    for hh in range(nhs):
        for u in range(nqb):
            qblk = q_off + pl.program_id(1) * nqb + u
            wstart = jnp.maximum(qblk - 1, 0)
            qT = qT_ref[hh, :, u * t:(u + 1) * t]
            s = _dot(kp_ref[rows(wstart, 2 * t), hh * LANES:(hh + 1) * LANES], qT)
            heads.append((hh, u, qblk, wstart, qT, s))

    chains = []
    for hh, u, qblk, wstart, qT, s in heads:
        s = jnp.where(krow_minus_qcol <= (qblk - wstart) * t, s, NEG_BIG)
        m = jnp.max(s, axis=0, keepdims=True)
        p = jnp.exp2(s - m)
        l = jnp.sum(p, axis=0, keepdims=True)
        acc = _dot(vT_ref[hh * hd:(hh + 1) * hd, rows(wstart, 2 * t)], p.astype(BF16))

        st = st_ref[hh]
        sel = blk_id == qblk
        pick = lambda r: jnp.sum(jnp.where(sel, st[r:r + 1, :], 0.0), axis=1, keepdims=True)
        bound = (ATT_NORM_SLACK * pick(ST_QNORM) * (st[ST_KNORM:ST_KNORM + 1, :] + pick(ST_KNORM))
                 + (pick(ST_CMAX) - st[ST_CMIN:ST_CMIN + 1, :]))
        live = jnp.logical_and(bound >= -ATT_SKIP_LOG * LOG2_E, blk_id < wstart)
        first = jnp.min(jnp.where(live, blk_id, wstart))
        chains.append((hh, u, qT, wstart, first, (m, l, acc)))

    for hh, u, qT, wstart, first, carry in chains:
        _, l, acc = lax.fori_loop(first, wstart, lambda j, c, qT=qT, hh=hh: online(j, c, qT, hh), carry)
        o_ref[hh * hd:(hh + 1) * hd, u * t:(u + 1) * t] = acc / l


def _attn(qT, kp, vT, stats):
    b, nh, _, tq_all = qT.shape
    tk_all = kp.shape[1]
    hd = vT.shape[1] // nh
    t = ATT_T
    nblk = tk_all // t
    q_off = (tk_all - tq_all) // t
    nqb = ATT_QBLOCKS if (tq_all // t) % ATT_QBLOCKS == 0 else 1
    nhs = 1 if nqb > 1 else nh
    ng = nh // nhs
    tq = nqb * t
    return pl.pallas_call(
        functools.partial(_attn_body, q_off, hd, nqb, nhs),
        grid=(b * ng, tq_all // tq),
        in_specs=[pl.BlockSpec((None, nhs, LANES, tq), lambda g, i: (g // ng, g % ng, 0, i)),
                  pl.BlockSpec((None, tk_all, nhs * LANES), lambda g, i: (g // ng, 0, g % ng)),
                  pl.BlockSpec((None, nhs * hd, tk_all), lambda g, i: (g // ng, g % ng, 0)),
                  pl.BlockSpec((None, nhs, SUBLANES, nblk), lambda g, i: (g // ng, g % ng, 0, 0))],
        out_specs=pl.BlockSpec((None, nhs * hd, tq), lambda g, i: (g // ng, g % ng, i)),
        out_shape=jax.ShapeDtypeStruct((b, nh * hd, tq_all), F32),
        compiler_params=_cparams(2),
        name="attn",
    )(qT, kp, vT, stats)


def _hgrn_body(nhg, q_ref, lf_ref, k_ref, v_ref, s0_ref, tri_ref, o_ref, sfin_ref,
               st_ref, b_ref, qf_ref, kf_ref, oi_ref, qb_ref, u_ref, dec_ref):
    tt, w = lf_ref.shape
    hd = w // nhg
    cl = HG_CHUNK
    nchunk = tt // cl

    @pl.when(pl.program_id(1) == 0)
    def _():
        for h in range(nhg):
            st_ref[h] = s0_ref[h].T

    tri = tri_ref[...]
    b = jnp.zeros((tt, w), F32)
    for part in _split3(lf_ref[...]):
        b = b + _dot(tri, part)
    qf = q_ref[...].astype(F32)
    kf = k_ref[...].astype(F32)
    for h in range(nhg):
        hs = slice(h * hd, (h + 1) * hd)
        b_ref[h], qf_ref[h], kf_ref[h] = b[:, hs], qf[:, hs], kf[:, hs]

    rid = lax.broadcasted_iota(jnp.int32, (cl, cl), 0)
    cid = lax.broadcasted_iota(jnp.int32, (cl, cl), 1)
    zpad = jnp.zeros((LANES - cl, hd), F32)
    worst = jnp.zeros((1, hd), F32)
    units = [(h, c) for h in range(nhg) for c in range(nchunk)]
    scores = {}
    for h, c in units:
        hs, sl = slice(h * hd, (h + 1) * hd), slice(c * cl, (c + 1) * cl)
        bc, qc, kc = b[sl, hs], qf[sl, hs], kf[sl, hs]
        vc = v_ref[sl, hs]
        r = bc[cl // 2 - 1:cl // 2, :]
        last = bc[cl - 1:cl, :]
        worst = jnp.maximum(worst, jnp.maximum(-r, r - last))
        qe = (qc * jnp.exp(bc - r)).astype(BF16)
        ke = (kc * jnp.exp(r - bc)).astype(BF16)
        scores[h, c] = _dot_nt(qe, ke)
        qb_ref[h, sl, :] = (qc * jnp.exp(bc)).astype(BF16)
        ke2 = jnp.concatenate([kc * jnp.exp(last - bc), zpad], axis=0).astype(BF16)
        vt = jnp.concatenate([vc.astype(F32), zpad], axis=0).T.astype(BF16)
        u_ref[h, c] = _dot(vt, ke2)
        dec_ref[h * nchunk + c:h * nchunk + c + 1, :] = jnp.exp(last)
    for h, c in units:
        hs, sl = slice(h * hd, (h + 1) * hd), slice(c * cl, (c + 1) * cl)
        a = jnp.where(rid >= cid, scores[h, c], 0.0)
        oi_ref[h, sl, :] = _dot(a.astype(BF16), v_ref[sl, hs])

    states = [st_ref[h] for h in range(nhg)]
    for c in range(nchunk):
        sl = slice(c * cl, (c + 1) * cl)
        for h in range(nhg):
            hs = slice(h * hd, (h + 1) * hd)
            o_ref[sl, hs] = _dot_nt(qb_ref[h, sl, :], states[h].astype(BF16))
            i = h * nchunk + c
            states[h] = states[h] * dec_ref[i:i + 1, :] + u_ref[h, c]
    for h in range(nhg):
        st_ref[h] = states[h]

    def row(t, _):
        base = pl.multiple_of((t // cl) * cl, cl)
        spos = base + lax.broadcasted_iota(jnp.int32, (cl, 1), 0)
        for h in range(nhg):
            hs = slice(h * hd, (h + 1) * hd)
            bt = b_ref[h, pl.ds(t, 1), :]
            qt = qf_ref[h, pl.ds(t, 1), :]
            e = jnp.exp(jnp.minimum(bt - b_ref[h, pl.ds(base, cl), :], 0.0))
            wgt = jnp.where(spos <= t, qt * kf_ref[h, pl.ds(base, cl), :] * e, 0.0)
            a = jnp.sum(wgt, axis=1, keepdims=True)
            vch = v_ref[pl.ds(base, cl), hs].astype(F32)
            oi_ref[h, pl.ds(t, 1), :] = jnp.sum(a * vch, axis=0, keepdims=True)
        return 0

    unsafe_rows = jnp.where(jnp.max(worst) <= HG_SAFE_LOG_RANGE, 0, tt)
    lax.fori_loop(0, unsafe_rows, row, 0)
    for h in range(nhg):
        hs = slice(h * hd, (h + 1) * hd)
        o_ref[:, hs] = o_ref[:, hs] + oi_ref[h]

    @pl.when(pl.program_id(1) == pl.num_programs(1) - 1)
    def _():
        for h in range(nhg):
            sfin_ref[h] = st_ref[h].T


def _hgrn(hq, hlf, hk, hi, s0, t_valid):
    b, _, w = hq.shape
    nhg, hd = s0.shape[1], s0.shape[2]
    tt = min(HG_TT, t_valid)
    cl = HG_CHUNK
    nchunk = tt // cl
    tri = jnp.asarray(np.kron(np.eye(nchunk, dtype=np.float32),
                              np.tril(np.ones((cl, cl), np.float32))), BF16)
    seq = lambda bi, s: (bi, s, 0)
    sidx = lambda bi, s: (bi, 0, 0, 0)
    blk = pl.BlockSpec((None, tt, w), seq)
    sblk = pl.BlockSpec((None, nhg, hd, hd), sidx)
    head_rows = pltpu.VMEM((nhg, tt, hd), F32)
    return pl.pallas_call(
        functools.partial(_hgrn_body, nhg),
        grid=(b, t_valid // tt),
        in_specs=[blk, blk, blk, blk, sblk, pl.BlockSpec(tri.shape, lambda bi, s: (0, 0))],
        out_specs=[blk, sblk],
        out_shape=[jax.ShapeDtypeStruct((b, t_valid, w), F32),
                   jax.ShapeDtypeStruct((b, nhg, hd, hd), F32)],
        scratch_shapes=[pltpu.VMEM((nhg, hd, hd), F32), head_rows, head_rows, head_rows, head_rows,
                        pltpu.VMEM((nhg, tt, hd), BF16), pltpu.VMEM((nhg, nchunk, hd, hd), F32),
                        pltpu.VMEM((nhg * nchunk, hd), F32)],
        compiler_params=_cparams(2),
        name="hgrn",
    )(hq, hlf, hk, hi, s0, tri)


def _merge_body(final, nhg, x_ref, oT_ref, fg_ref, ho_ref, hg_ref, gn_ref, w1_ref, w2_ref, fgain_ref,
                y_ref):
    fy = (oT_ref[...].T * _silu(fg_ref[...].astype(F32))).astype(BF16)
    ho = ho_ref[...]
    hw = ho.shape[1] // nhg
    normed = []
    for h in range(nhg):
        oh = ho[:, h * hw:(h + 1) * hw]
        inv = lax.rsqrt(jnp.mean(oh * oh, axis=-1, keepdims=True) + NORM_EPS)
        normed.append(oh * inv * gn_ref[:, h * hw:(h + 1) * hw])
    hy = (jnp.concatenate(normed, axis=1) * _silu(hg_ref[...].astype(F32))).astype(BF16)
    out = x_ref[...] + _dot(fy, w1_ref[...]) + _dot(hy, w2_ref[...])
    if final:
        inv = lax.rsqrt(jnp.mean(out * out, axis=-1, keepdims=True) + NORM_EPS)
        out = (out * inv) * fgain_ref[...]
    y_ref[...] = out


def _merge(x, oT, fg, ho, hg, gn, w1, w2, fgain, nhg, final):
    b, t, d = x.shape
    fw = fg.shape[2]
    tm = min(PROJ_TM, t)
    seq = lambda bi, s: (bi, s, 0)
    const = lambda bi, s: (0, 0)
    blk = pl.BlockSpec((None, tm, fw), seq)
    return pl.pallas_call(
        functools.partial(_merge_body, final, nhg),
        grid=(b, t // tm),
        in_specs=[pl.BlockSpec((None, tm, d), seq),
                  pl.BlockSpec((None, fw, tm), lambda bi, s: (bi, 0, s)),
                  blk, blk, blk, pl.BlockSpec(gn.shape, const),
                  pl.BlockSpec(w1.shape, const), pl.BlockSpec(w2.shape, const),
                  pl.BlockSpec(fgain.shape, const)],
        out_specs=pl.BlockSpec((None, tm, d), seq),
        out_shape=jax.ShapeDtypeStruct((b, t, d), F32),
        compiler_params=_cparams(2),
        name="merge",
    )(x, oT, fg, ho, hg, gn, w1, w2, fgain)


def _path(x, t_valid, cache, states0, wts):
    nh, hd, nhg = wts["nh"], wts["hd"], wts["nhg"]
    depth = wts["hl"].shape[0]
    t = x.shape[1]
    feats = None
    s_fin = []
    for layer in range(depth):
        if cache is None:
            qT, kT, vT, lfT, fg, hq, hlf, hk, hi, hg, kp, vTb, stats = _proj(x, wts, layer, feats, True)
            feats = (kT, vT, lfT)
        else:
            q, kT, vT, lfT, fg, hq, hlf, hk, hi, hg = _proj(x, wts, layer, feats, False)
            feats = (kT, vT, lfT)
            qT, kp, vTb, stats = _prep(q, feats, cache, layer, nh, hd)
        stats = jnp.transpose(stats, (0, 3, 2, 1))[:, :nh]
        oT = _attn(qT, kp, vTb, stats)
        ho, s_new = _hgrn(hq, hlf, hk, hi, states0[layer], t_valid)
        if t_valid < t:
            ho = jnp.pad(ho, ((0, 0), (0, t - t_valid), (0, 0)))
        x = _merge(x, oT, fg, ho, hg, wts["gn"][layer], wts["w1"][layer], wts["w2"][layer],
                   wts["fgain"], nhg, layer == depth - 1)
        s_fin.append(s_new)
    return x, feats, jnp.stack(s_fin)


def _caches_out(feats, t_valid, nh, hd):
    kT, vT, lfT = (a[..., :t_valid] for a in feats)
    depth, b = kT.shape[:2]
    to_tok = lambda a: jnp.transpose(a.reshape(depth, b, nh, hd, t_valid), (0, 1, 4, 2, 3))
    return to_tok(kT), to_tok(vT), jnp.transpose(lfT, (0, 1, 3, 2))


def kernel(x_prompt, x_sample, cache_k, cache_v, cache_logf, state_hgrn,
           norm_g, w_in, fox_b_f, hg_lower, hg_norm_g, w_out, final_g):
    depth, d = norm_g.shape
    nh = fox_b_f.shape[1]
    fw = w_out.shape[1] // 2
    hd = fw // nh
    nhg = state_hgrn.shape[2]
    assert LANES % hd == 0 and nh % (LANES // hd) == 0 and 3 * nh <= LANES
    o = 3 * fw
    wm = jnp.concatenate([w_in[:, :, :o], w_in[:, :, o + nh:]], axis=2).astype(BF16)
    wf = jnp.pad(w_in[:, :, o:o + nh], ((0, 0), (0, 0), (0, LANES - nh))).astype(BF16)
    wts = dict(
        nh=nh, hd=hd, nhg=nhg, wm=wm, wf=wf,
        g=norm_g.reshape(depth, 1, d),
        bf=jnp.pad(fox_b_f, ((0, 0), (0, LANES - nh))).reshape(depth, 1, LANES),
        hl=hg_lower, gn=hg_norm_g.reshape(depth, 1, -1),
        w1=w_out[:, :fw, :].astype(BF16), w2=w_out[:, fw:, :].astype(BF16),
        fgain=final_g.reshape(1, d))

    bp, sp, _ = x_prompt.shape
    assert sp % PROJ_TM == 0 and PROJ_TM % ATT_T == 0 and sp >= 2 * ATT_T
    zero_states = jnp.zeros((depth, bp) + state_hgrn.shape[2:], F32)
    y_p, feats_p, s_p = _path(x_prompt, sp, None, zero_states, wts)

    bs, ts_, _ = x_sample.shape
    past = cache_k.shape[2]
    assert past % ATT_T == 0 and ts_ % HG_CHUNK == 0
    t_pad = -(-ts_ // ATT_T) * ATT_T
    xs = jnp.pad(x_sample, ((0, 0), (0, t_pad - ts_), (0, 0)))
    to_featT = lambda a: jnp.transpose(a, (0, 1, 3, 4, 2)).reshape(depth, bs, fw, past)
    cache = (to_featT(cache_k), to_featT(cache_v), jnp.transpose(cache_logf, (0, 1, 3, 2)))
    y_s, feats_s, s_s = _path(xs, ts_, cache, state_hgrn, wts)

    k_p, v_p, lf_p = _caches_out(feats_p, sp, nh, hd)
    k_s, v_s, lf_s = _caches_out(feats_s, ts_, nh, hd)
    return (y_p, y_s[:, :ts_], k_p, v_p, lf_p, s_p, k_s, v_s, lf_s, s_s)
```

```python
import functools

import jax
import jax.numpy as jnp
import numpy as np
from jax import lax
from jax.experimental import pallas as pl
from jax.experimental.pallas import tpu as pltpu

F32 = jnp.float32
BF16 = jnp.bfloat16

NORM_EPS = 1e-6
HG_CHUNK = 64
HG_SAFE_LOG_RANGE = 60.0
LANES = 128
SUBLANES = 8
NEG_BIG = -1e30
ATT_SKIP_LOG = 110.0
ATT_NORM_SLACK = 1.02
LOG2_E = 1.4426950408889634
VMEM_LIMIT = 56 * 1024 * 1024

PROJ_TM = 512
ATT_T = 256
ATT_QBLOCKS = 8
HG_TT = 256

ST_CMAX, ST_CMIN, ST_KNORM, ST_QNORM = 0, 1, 2, 3


def _cparams(n_axes):
    return pltpu.CompilerParams(dimension_semantics=("arbitrary",) * n_axes,
                                vmem_limit_bytes=VMEM_LIMIT)


def _log_sigmoid(x):
    return jnp.minimum(x, 0.0) - jnp.log1p(jnp.exp(-jnp.abs(x)))


def _silu(x):
    return x / (1.0 + jnp.exp(-x))


def _split3(a):
    hi = a.astype(BF16)
    r = a - hi.astype(F32)
    mid = r.astype(BF16)
    lo = (r - mid.astype(F32)).astype(BF16)
    return hi, mid, lo


def _dot(a, b):
    return jnp.dot(a, b, preferred_element_type=F32)


def _dot_nt(a, b):
    return lax.dot_general(a, b, (((1,), (1,)), ((), ())), preferred_element_type=F32)


def _dot_tn(a, b):
    return lax.dot_general(a, b, (((0,), (0,)), ((), ())), preferred_element_type=F32)


def _cumulative(lft, tri, carry_ref):
    n = lft.shape[1]
    ct = carry_ref[:, 0:1]
    for part in _split3(lft):
        ct = ct + _dot(part, tri)
    carry_ref[...] = jnp.broadcast_to(ct[:, n - 1:n], carry_ref.shape)
    return ct


def _operand_tile(xT, parts, h, hd, is_query):
    n = xT.shape[1]
    r = lax.broadcasted_iota(jnp.int32, (SUBLANES, n), 0)
    hi, mid, lo = (jnp.broadcast_to(p[h:h + 1, :].astype(F32), (SUBLANES, n)) for p in parts)
    if is_query:
        rows = jnp.where(r == 0, hi, jnp.where(r == 1, mid, jnp.where(r == 2, lo,
                                                                       jnp.where(r < 6, 1.0, 0.0))))
    else:
        rows = jnp.where(r < 3, 1.0, jnp.where(r == 3, -hi, jnp.where(r == 4, -mid,
                                                                       jnp.where(r == 5, -lo, 0.0))))
    pad = jnp.zeros((LANES - hd - SUBLANES, n), F32)
    return jnp.concatenate([xT[h * hd:(h + 1) * hd, :], rows, pad], axis=0).astype(BF16)


def _sq_norms(xT, nh, hd):
    xr = xT.astype(BF16).astype(F32)
    return jnp.concatenate([jnp.sum(xr[h * hd:(h + 1) * hd, :] ** 2, axis=0, keepdims=True)
                            for h in range(nh)], axis=0)


def _block_stats(c2, k2, q2):
    lane = lax.broadcasted_iota(jnp.int32, (c2.shape[0], LANES), 1)
    top = lambda a: jnp.max(a, axis=1, keepdims=True)
    qn = jnp.zeros_like(top(k2)) if q2 is None else jnp.sqrt(top(q2))
    return jnp.where(lane == ST_CMAX, top(c2), jnp.where(lane == ST_CMIN, -top(-c2), jnp.where(
        lane == ST_KNORM, jnp.sqrt(top(k2)), jnp.where(lane == ST_QNORM, qn, 0.0))))


def _proj_body(layer, nh, hd, fused, x_ref, g_ref, wm_ref, wf_ref, bf_ref, hl_ref, *rest):
    if fused:
        tri_ref = rest[0]
        (q_ref, kT_ref, vT_ref, lfT_ref, fg_ref, hq_ref, hlf_ref, hk_ref, hi_ref, hg_ref,
         kp_ref, vTb_ref, st_ref, carry_ref) = rest[-14:]
    else:
        (q_ref, kT_ref, vT_ref, lfT_ref, fg_ref, hq_ref, hlf_ref, hk_ref, hi_ref, hg_ref) = rest[-10:]
    fw = nh * hd
    scale = float(hd) ** -0.5 * LOG2_E
    x = x_ref[...]
    inv = lax.rsqrt(jnp.mean(x * x, axis=-1, keepdims=True) + NORM_EPS)
    hb = ((x * inv) * g_ref[...]).astype(BF16)

    def col(i):
        return _dot(hb, wm_ref[:, i * fw:(i + 1) * fw])

    lf = _log_sigmoid(_dot(hb, wf_ref[...]) + bf_ref[...])
    lft = lf.T[:nh, :]
    lfT_ref[...] = lft
    qT = (col(0) * scale).T
    kT = col(1).T
    vT = col(2).T
    kT_ref[...] = kT
    vT_ref[...] = vT
    if fused:
        @pl.when(pl.program_id(1) == 0)
        def _():
            carry_ref[...] = jnp.zeros_like(carry_ref)

        vTb_ref[...] = vT.astype(BF16)
        c2 = _cumulative(lft, tri_ref[...], carry_ref) * LOG2_E
        parts = _split3(c2)
        for h in range(nh):
            q_ref[h] = _operand_tile(qT, parts, h, hd, True)
            kp_ref[h] = _operand_tile(kT, parts, h, hd, False)
        k2, q2 = _sq_norms(kT, nh, hd), _sq_norms(qT, nh, hd)
        for i in range(x.shape[0] // ATT_T):
            sl = slice(i * ATT_T, (i + 1) * ATT_T)
            st_ref[i] = _block_stats(c2[:, sl], k2[:, sl], q2[:, sl])
    else:
        q_ref[...] = qT.astype(BF16)

    fg_ref[...] = col(3).astype(BF16)
    hq_ref[...] = _silu(col(4)).astype(BF16)

    rows = [hl_ref[i:i + 1, :] for i in range(hl_ref.shape[0])]
    mx = functools.reduce(jnp.maximum, rows)
    es = [jnp.exp(r - mx) for r in rows]
    tot = functools.reduce(lambda a, b: a + b, es)
    lb = jnp.zeros_like(mx)
    for i in range(1, layer + 1):
        lb = lb + es[i] / tot

    zf = col(5)
    a = jnp.log(lb)
    b = jnp.log1p(-lb) + _log_sigmoid(zf)
    hlf_ref[...] = jnp.maximum(a, b) + jnp.log1p(jnp.exp(-jnp.abs(a - b)))
    hk_ref[...] = ((1.0 - lb) / (1.0 + jnp.exp(zf))).astype(BF16)
    hi_ref[...] = col(6).astype(BF16)
    hg_ref[...] = col(7).astype(BF16)


def _proj(x, wts, layer, prev, fused):
    b, t, d = x.shape
    nh, hd = wts["nh"], wts["hd"]
    depth = wts["hl"].shape[0]
    fw = nh * hd
    tm = min(PROJ_TM, t)
    seq = lambda bi, s: (bi, s, 0)
    posT = lambda bi, s: (bi, 0, s)
    headT = lambda bi, s: (bi, 0, 0, s)
    featT = lambda bi, s: (layer, bi, 0, s)
    c2 = lambda bi, s: (0, 0)
    g, wm, wf, bfp = wts["g"][layer], wts["wm"][layer], wts["wf"][layer], wts["bf"][layer]
    args = [x, g, wm, wf, bfp, wts["hl"]]
    if fused:
        args.append(jnp.asarray(np.triu(np.ones((tm, tm), np.float32)), BF16))
    in_specs = [pl.BlockSpec((None, tm, d), seq)] + [pl.BlockSpec(a.shape, c2) for a in args[1:]]
    aliases = {}
    if prev is not None:
        for i, a in enumerate(prev):
            in_specs.append(pl.BlockSpec(memory_space=pl.ANY))
            aliases[len(args)] = 1 + i
            args.append(a)
    tok = pl.BlockSpec((None, tm, fw), seq)
    b16o = jax.ShapeDtypeStruct((b, t, fw), BF16)
    feat = pl.BlockSpec((None, None, fw, tm), featT)
    out_specs = [pl.BlockSpec((None, fw, tm), posT), feat, feat, pl.BlockSpec((None, None, nh, tm), featT),
                 tok, tok, tok, tok, tok, tok]
    out_shape = [jax.ShapeDtypeStruct((b, fw, t), BF16), jax.ShapeDtypeStruct((depth, b, fw, t), F32),
                 jax.ShapeDtypeStruct((depth, b, fw, t), F32),
                 jax.ShapeDtypeStruct((depth, b, nh, t), F32),
                 b16o, b16o, jax.ShapeDtypeStruct((b, t, fw), F32), b16o, b16o, b16o]
    scratch = []
    if fused:
        operand = pl.BlockSpec((None, nh, LANES, tm), headT)
        operand_shape = jax.ShapeDtypeStruct((b, nh, LANES, t), BF16)
        out_specs[0], out_shape[0] = operand, operand_shape
        out_specs += [operand, pl.BlockSpec((None, fw, tm), posT),
                      pl.BlockSpec((None, tm // ATT_T, nh, LANES), lambda bi, s: (bi, s, 0, 0))]
        out_shape += [operand_shape, jax.ShapeDtypeStruct((b, fw, t), BF16),
                      jax.ShapeDtypeStruct((b, t // ATT_T, nh, LANES), F32)]
        scratch = [pltpu.VMEM((nh, LANES), F32)]
    return pl.pallas_call(
        functools.partial(_proj_body, layer, nh, hd, fused),
        grid=(b, t // tm),
        in_specs=in_specs,
        out_specs=out_specs,
        out_shape=out_shape,
        scratch_shapes=scratch,
        input_output_aliases=aliases,
        compiler_params=_cparams(2),
        name="proj",
    )(*args)


def _prep_body(nh, hd, n_past, q_ref, kc_ref, vc_ref, lfc_ref, kn_ref, vn_ref, lfn_ref, tri_ref,
               qT_ref, kp_ref, vT_ref, st_ref, carry_ref):
    is_new = pl.program_id(1) >= n_past
    kT = jnp.where(is_new, kn_ref[...], kc_ref[...])
    vT = jnp.where(is_new, vn_ref[...], vc_ref[...])
    lft = jnp.where(is_new, lfn_ref[...], lfc_ref[...])

    @pl.when(pl.program_id(1) == 0)
    def _():
        carry_ref[...] = jnp.zeros_like(carry_ref)

    c2 = _cumulative(lft, tri_ref[...], carry_ref) * LOG2_E
    parts = _split3(c2)
    for h in range(nh):
        kp_ref[h] = _operand_tile(kT, parts, h, hd, False)
    vT_ref[...] = vT.astype(BF16)
    k2 = _sq_norms(kT, nh, hd)
    st_ref[...] = _block_stats(c2, k2, None)

    @pl.when(is_new)
    def _():
        qT = q_ref[...].astype(F32)
        for h in range(nh):
            qT_ref[h] = _operand_tile(qT, parts, h, hd, True)
        st_ref[...] = _block_stats(c2, k2, _sq_norms(qT, nh, hd))


def _prep(qT, new, cache, layer, nh, hd):
    b, fw, tq = qT.shape
    ts = ATT_T
    n_past = cache[0].shape[3] // ts
    nblk = n_past + tq // ts
    tri = jnp.asarray(np.triu(np.ones((ts, ts), np.float32)), BF16)
    new_blk = lambda s: jnp.maximum(s - n_past, 0)
    featT_new = lambda bi, s: (layer, bi, 0, new_blk(s))
    featT_old = lambda bi, s: (layer, bi, 0, jnp.minimum(s, n_past - 1))

    def feat_specs(idx):
        return [pl.BlockSpec((None, None, fw, ts), idx), pl.BlockSpec((None, None, fw, ts), idx),
                pl.BlockSpec((None, None, nh, ts), idx)]

    in_specs = ([pl.BlockSpec((None, fw, ts), lambda bi, s: (bi, 0, new_blk(s)))]
                + feat_specs(featT_old) + feat_specs(featT_new)
                + [pl.BlockSpec(tri.shape, lambda bi, s: (0, 0))])
    return pl.pallas_call(
        functools.partial(_prep_body, nh, hd, n_past),
        grid=(b, nblk),
        in_specs=in_specs,
        out_specs=[pl.BlockSpec((None, nh, LANES, ts), lambda bi, s: (bi, 0, 0, new_blk(s))),
                   pl.BlockSpec((None, nh, LANES, ts), lambda bi, s: (bi, 0, 0, s)),
                   pl.BlockSpec((None, fw, ts), lambda bi, s: (bi, 0, s)),
                   pl.BlockSpec((None, None, nh, LANES), lambda bi, s: (bi, s, 0, 0))],
        out_shape=[jax.ShapeDtypeStruct((b, nh, LANES, tq), BF16),
                   jax.ShapeDtypeStruct((b, nh, LANES, nblk * ts), BF16),
                   jax.ShapeDtypeStruct((b, fw, nblk * ts), BF16),
                   jax.ShapeDtypeStruct((b, nblk, nh, LANES), F32)],
        scratch_shapes=[pltpu.VMEM((nh, LANES), F32)],
        compiler_params=_cparams(2),
        name="prep",
    )(qT, *cache, *new, tri)


def _attn_body(q_off, hd, nqb, nhs, qT_ref, kp_ref, vT_ref, st_ref, o_ref):
    t = ATT_T
    nblk = st_ref.shape[2]
    blk_id = lax.broadcasted_iota(jnp.int32, (1, nblk), 1)
    krow_minus_qcol = (lax.broadcasted_iota(jnp.int32, (2 * t, t), 0)
                       - lax.broadcasted_iota(jnp.int32, (2 * t, t), 1))
    cols = lambda j, n: pl.ds(pl.multiple_of(j * t, t), n)

    def online(j, carry, qT, hh):
        m, l, acc = carry
        s = _dot_tn(kp_ref[hh, :, cols(j, t)], qT)
        m_new = jnp.maximum(m, jnp.max(s, axis=0, keepdims=True))
        alpha = jnp.exp2(m - m_new)
        p = jnp.exp2(s - m_new)
        l = alpha * l + jnp.sum(p, axis=0, keepdims=True)
        acc = alpha * acc + _dot(vT_ref[hh * hd:(hh + 1) * hd, cols(j, t)], p.astype(BF16))
        return m_new, l, acc

    heads = []
    for hh in range(nhs):
        for u in range(nqb):
            qblk = q_off + pl.program_id(1) * nqb + u
            wstart = jnp.maximum(qblk - 1, 0)
            qT = qT_ref[hh, :, u * t:(u + 1) * t]
            s = _dot_tn(kp_ref[hh, :, cols(wstart, 2 * t)], qT)
            heads.append((hh, u, qblk, wstart, qT, s))

    chains = []
    for hh, u, qblk, wstart, qT, s in heads:
        s = jnp.where(krow_minus_qcol <= (qblk - wstart) * t, s, NEG_BIG)
        m = jnp.max(s, axis=0, keepdims=True)
        p = jnp.exp2(s - m)
        l = jnp.sum(p, axis=0, keepdims=True)
        acc = _dot(vT_ref[hh * hd:(hh + 1) * hd, cols(wstart, 2 * t)], p.astype(BF16))

        st = st_ref[hh]
        sel = blk_id == qblk
        pick = lambda r: jnp.sum(jnp.where(sel, st[r:r + 1, :], 0.0), axis=1, keepdims=True)
        bound = (ATT_NORM_SLACK * pick(ST_QNORM) * (st[ST_KNORM:ST_KNORM + 1, :] + pick(ST_KNORM))
                 + (pick(ST_CMAX) - st[ST_CMIN:ST_CMIN + 1, :]))
        live = jnp.logical_and(bound >= -ATT_SKIP_LOG * LOG2_E, blk_id < wstart)
        first = jnp.min(jnp.where(live, blk_id, wstart))
        chains.append((hh, u, qT, wstart, first, (m, l, acc)))

    if nqb == 1:
        wstart = chains[0][3]
        first = functools.reduce(jnp.minimum, [ch[4] for ch in chains])

        def body(j, carries):
            return tuple(online(j, c, ch[2], ch[0]) for c, ch in zip(carries, chains))

        carries = lax.fori_loop(first, wstart, body, tuple(ch[5] for ch in chains))
    else:
        carries = [lax.fori_loop(first, wstart, lambda j, c, qT=qT, hh=hh: online(j, c, qT, hh), carry)
                   for hh, u, qT, wstart, first, carry in chains]
    for (hh, u, _, _, _, _), (_, l, acc) in zip(chains, carries):
        o_ref[hh * hd:(hh + 1) * hd, u * t:(u + 1) * t] = (acc / l).astype(o_ref.dtype)


def _attn(qT, kp, vT, stats):
    b, nh, _, tq_all = qT.shape
    tk_all = kp.shape[3]
    hd = vT.shape[1] // nh
    t = ATT_T
    nblk = tk_all // t
    q_off = (tk_all - tq_all) // t
    nqb = ATT_QBLOCKS if (tq_all // t) % ATT_QBLOCKS == 0 else 1
    nhs = 1 if nqb > 1 else nh
    ng = nh // nhs
    tq = nqb * t
    head_blk = lambda g, i: (g // ng, g % ng, 0, 0)
    return pl.pallas_call(
        functools.partial(_attn_body, q_off, hd, nqb, nhs),
        grid=(b * ng, tq_all // tq),
        in_specs=[pl.BlockSpec((None, nhs, LANES, tq), lambda g, i: (g // ng, g % ng, 0, i)),
                  pl.BlockSpec((None, nhs, LANES, tk_all), head_blk),
                  pl.BlockSpec((None, nhs * hd, tk_all), lambda g, i: (g // ng, g % ng, 0)),
                  pl.BlockSpec((None, nhs, SUBLANES, nblk), head_blk)],
        out_specs=pl.BlockSpec((None, nhs * hd, tq), lambda g, i: (g // ng, g % ng, i)),
        out_shape=jax.ShapeDtypeStruct((b, nh * hd, tq_all), BF16),
        compiler_params=_cparams(2),
        name="attn",
    )(qT, kp, vT, stats)


def _hgrn_body(nhg, q_ref, lf_ref, k_ref, v_ref, s0_ref, tri_ref, o_ref, sfin_ref,
               st_ref, b_ref, qf_ref, kf_ref, oi_ref, ox_ref, qb_ref, u_ref, dec_ref):
    tt, w = lf_ref.shape
    hd = w // nhg
    cl = HG_CHUNK
    nchunk = tt // cl

    @pl.when(pl.program_id(1) == 0)
    def _():
        for h in range(nhg):
            st_ref[h] = s0_ref[h].T

    tri = tri_ref[...]
    b = jnp.zeros((tt, w), F32)
    for part in _split3(lf_ref[...]):
        b = b + _dot(tri, part)
    qf = q_ref[...].astype(F32)
    kf = k_ref[...].astype(F32)
    for h in range(nhg):
        hs = slice(h * hd, (h + 1) * hd)
        b_ref[h], qf_ref[h], kf_ref[h] = b[:, hs], qf[:, hs], kf[:, hs]

    rid = lax.broadcasted_iota(jnp.int32, (cl, cl), 0)
    cid = lax.broadcasted_iota(jnp.int32, (cl, cl), 1)
    zpad = jnp.zeros((LANES - cl, hd), F32)
    worst = jnp.zeros((1, hd), F32)
    units = [(h, c) for h in range(nhg) for c in range(nchunk)]
    scores = {}
    for h, c in units:
        hs, sl = slice(h * hd, (h + 1) * hd), slice(c * cl, (c + 1) * cl)
        bc, qc, kc = b[sl, hs], qf[sl, hs], kf[sl, hs]
        vc = v_ref[sl, hs]
        r = bc[cl // 2 - 1:cl // 2, :]
        last = bc[cl - 1:cl, :]
        worst = jnp.maximum(worst, jnp.maximum(-r, r - last))
        qe = (qc * jnp.exp(bc - r)).astype(BF16)
        ke = (kc * jnp.exp(r - bc)).astype(BF16)
        scores[h, c] = _dot_nt(qe, ke)
        qb_ref[h, sl, :] = (qc * jnp.exp(bc)).astype(BF16)
        ke2 = jnp.concatenate([kc * jnp.exp(last - bc), zpad], axis=0).astype(BF16)
        vt = jnp.concatenate([vc.astype(F32), zpad], axis=0).T.astype(BF16)
        u_ref[h, c] = _dot(vt, ke2)
        dec_ref[h * nchunk + c:h * nchunk + c + 1, :] = jnp.exp(last)
    for h, c in units:
        hs, sl = slice(h * hd, (h + 1) * hd), slice(c * cl, (c + 1) * cl)
        a = jnp.where(rid >= cid, scores[h, c], 0.0)
        oi_ref[h, sl, :] = _dot(a.astype(BF16), v_ref[sl, hs])

    states = [st_ref[h] for h in range(nhg)]
    for c in range(nchunk):
        sl = slice(c * cl, (c + 1) * cl)
        for h in range(nhg):
            ox_ref[h, sl, :] = _dot_nt(qb_ref[h, sl, :], states[h].astype(BF16))
            i = h * nchunk + c
            states[h] = states[h] * dec_ref[i:i + 1, :] + u_ref[h, c]
    for h in range(nhg):
        st_ref[h] = states[h]

    def row(t, _):
        base = pl.multiple_of((t // cl) * cl, cl)
        spos = base + lax.broadcasted_iota(jnp.int32, (cl, 1), 0)
        for h in range(nhg):
            hs = slice(h * hd, (h + 1) * hd)
            bt = b_ref[h, pl.ds(t, 1), :]
            qt = qf_ref[h, pl.ds(t, 1), :]
            e = jnp.exp(jnp.minimum(bt - b_ref[h, pl.ds(base, cl), :], 0.0))
            wgt = jnp.where(spos <= t, qt * kf_ref[h, pl.ds(base, cl), :] * e, 0.0)
            a = jnp.sum(wgt, axis=1, keepdims=True)
            vch = v_ref[pl.ds(base, cl), hs].astype(F32)
            oi_ref[h, pl.ds(t, 1), :] = jnp.sum(a * vch, axis=0, keepdims=True)
        return 0

    unsafe_rows = jnp.where(jnp.max(worst) <= HG_SAFE_LOG_RANGE, 0, tt)
    lax.fori_loop(0, unsafe_rows, row, 0)
    for h in range(nhg):
        o_ref[:, h * hd:(h + 1) * hd] = (ox_ref[h] + oi_ref[h]).astype(o_ref.dtype)

    @pl.when(pl.program_id(1) == pl.num_programs(1) - 1)
    def _():
        for h in range(nhg):
            sfin_ref[h] = st_ref[h].T


def _hgrn(hq, hlf, hk, hi, s0, t_valid):
    b, _, w = hq.shape
    nhg, hd = s0.shape[1], s0.shape[2]
    tt = min(HG_TT, t_valid)
    cl = HG_CHUNK
    nchunk = tt // cl
    tri = jnp.asarray(np.kron(np.eye(nchunk, dtype=np.float32),
                              np.tril(np.ones((cl, cl), np.float32))), BF16)
    seq = lambda bi, s: (bi, s, 0)
    sidx = lambda bi, s: (bi, 0, 0, 0)
    blk = pl.BlockSpec((None, tt, w), seq)
    sblk = pl.BlockSpec((None, nhg, hd, hd), sidx)
    head_rows = pltpu.VMEM((nhg, tt, hd), F32)
    return pl.pallas_call(
        functools.partial(_hgrn_body, nhg),
        grid=(b, t_valid // tt),
        in_specs=[blk, blk, blk, blk, sblk, pl.BlockSpec(tri.shape, lambda bi, s: (0, 0))],
        out_specs=[blk, sblk],
        out_shape=[jax.ShapeDtypeStruct((b, t_valid, w), BF16),
                   jax.ShapeDtypeStruct((b, nhg, hd, hd), F32)],
        scratch_shapes=[pltpu.VMEM((nhg, hd, hd), F32), head_rows, head_rows, head_rows, head_rows,
                        head_rows, pltpu.VMEM((nhg, tt, hd), BF16),
                        pltpu.VMEM((nhg, nchunk, hd, hd), F32), pltpu.VMEM((nhg * nchunk, hd), F32)],
        compiler_params=_cparams(2),
        name="hgrn",
    )(hq, hlf, hk, hi, s0, tri)


def _merge_body(final, nhg, x_ref, oT_ref, fg_ref, ho_ref, hg_ref, gn_ref, w1_ref, w2_ref, fgain_ref,
                y_ref):
    fy = (oT_ref[...].astype(F32).T * _silu(fg_ref[...].astype(F32))).astype(BF16)
    ho = ho_ref[...].astype(F32)
    hw = ho.shape[1] // nhg
    normed = []
    for h in range(nhg):
        oh = ho[:, h * hw:(h + 1) * hw]
        inv = lax.rsqrt(jnp.mean(oh * oh, axis=-1, keepdims=True) + NORM_EPS)
        normed.append(oh * inv * gn_ref[:, h * hw:(h + 1) * hw])
    hy = (jnp.concatenate(normed, axis=1) * _silu(hg_ref[...].astype(F32))).astype(BF16)
    out = x_ref[...] + _dot(fy, w1_ref[...]) + _dot(hy, w2_ref[...])
    if final:
        inv = lax.rsqrt(jnp.mean(out * out, axis=-1, keepdims=True) + NORM_EPS)
        out = (out * inv) * fgain_ref[...]
    y_ref[...] = out


def _merge(x, oT, fg, ho, hg, gn, w1, w2, fgain, nhg, final):
    b, t, d = x.shape
    fw = fg.shape[2]
    tm = min(PROJ_TM, t)
    seq = lambda bi, s: (bi, s, 0)
    const = lambda bi, s: (0, 0)
    blk = pl.BlockSpec((None, tm, fw), seq)
    return pl.pallas_call(
        functools.partial(_merge_body, final, nhg),
        grid=(b, t // tm),
        in_specs=[pl.BlockSpec((None, tm, d), seq),
                  pl.BlockSpec((None, fw, tm), lambda bi, s: (bi, 0, s)),
                  blk, blk, blk, pl.BlockSpec(gn.shape, const),
                  pl.BlockSpec(w1.shape, const), pl.BlockSpec(w2.shape, const),
                  pl.BlockSpec(fgain.shape, const)],
        out_specs=pl.BlockSpec((None, tm, d), seq),
        out_shape=jax.ShapeDtypeStruct((b, t, d), F32),
        compiler_params=_cparams(2),
        name="merge",
    )(x, oT, fg, ho, hg, gn, w1, w2, fgain)


def _path(x, t_valid, cache, states0, wts):
    nh, hd, nhg = wts["nh"], wts["hd"], wts["nhg"]
    depth = wts["hl"].shape[0]
    t = x.shape[1]
    feats = None
    s_fin = []
    for layer in range(depth):
        if cache is None:
            qT, kT, vT, lfT, fg, hq, hlf, hk, hi, hg, kp, vTb, stats = _proj(x, wts, layer, feats, True)
            feats = (kT, vT, lfT)
        else:
            qT, kT, vT, lfT, fg, hq, hlf, hk, hi, hg = _proj(x, wts, layer, feats, False)
            feats = (kT, vT, lfT)
            qT, kp, vTb, stats = _prep(qT, feats, cache, layer, nh, hd)
        stats = jnp.transpose(stats[..., :SUBLANES], (0, 2, 3, 1))
        oT = _attn(qT, kp, vTb, stats)
        ho, s_new = _hgrn(hq, hlf, hk, hi, states0[layer], t_valid)
        if t_valid < t:
            ho = jnp.pad(ho, ((0, 0), (0, t - t_valid), (0, 0)))
        x = _merge(x, oT, fg, ho, hg, wts["gn"][layer], wts["w1"][layer], wts["w2"][layer],
                   wts["fgain"], nhg, layer == depth - 1)
        s_fin.append(s_new)
    return x, feats, jnp.stack(s_fin)


def _caches_out(feats, t_valid, nh, hd):
    kT, vT, lfT = (a[..., :t_valid] for a in feats)
    depth, b = kT.shape[:2]
    to_tok = lambda a: jnp.transpose(a.reshape(depth, b, nh, hd, t_valid), (0, 1, 4, 2, 3))
    return to_tok(kT), to_tok(vT), jnp.transpose(lfT, (0, 1, 3, 2))


def kernel(x_prompt, x_sample, cache_k, cache_v, cache_logf, state_hgrn,
           norm_g, w_in, fox_b_f, hg_lower, hg_norm_g, w_out, final_g):
    depth, d = norm_g.shape
    nh = fox_b_f.shape[1]
    fw = w_out.shape[1] // 2
    hd = fw // nh
    nhg = state_hgrn.shape[2]
    assert hd % SUBLANES == 0 and hd + SUBLANES <= LANES and nh <= SUBLANES
    o = 3 * fw
    wm = jnp.concatenate([w_in[:, :, :o], w_in[:, :, o + nh:]], axis=2).astype(BF16)
    wf = jnp.pad(w_in[:, :, o:o + nh], ((0, 0), (0, 0), (0, LANES - nh))).astype(BF16)
    wts = dict(
        nh=nh, hd=hd, nhg=nhg, wm=wm, wf=wf,
        g=norm_g.reshape(depth, 1, d),
        bf=jnp.pad(fox_b_f, ((0, 0), (0, LANES - nh))).reshape(depth, 1, LANES),
        hl=hg_lower, gn=hg_norm_g.reshape(depth, 1, -1),
        w1=w_out[:, :fw, :].astype(BF16), w2=w_out[:, fw:, :].astype(BF16),
        fgain=final_g.reshape(1, d))

    bp, sp, _ = x_prompt.shape
    assert sp % PROJ_TM == 0 and PROJ_TM % ATT_T == 0 and sp >= 2 * ATT_T
    zero_states = jnp.zeros((depth, bp) + state_hgrn.shape[2:], F32)
    y_p, feats_p, s_p = _path(x_prompt, sp, None, zero_states, wts)

    bs, ts_, _ = x_sample.shape
    past = cache_k.shape[2]
    assert past % ATT_T == 0 and ts_ % HG_CHUNK == 0
    t_pad = -(-ts_ // ATT_T) * ATT_T
    xs = jnp.pad(x_sample, ((0, 0), (0, t_pad - ts_), (0, 0)))
    to_featT = lambda a: jnp.transpose(a, (0, 1, 3, 4, 2)).reshape(depth, bs, fw, past)
    cache = (to_featT(cache_k), to_featT(cache_v), jnp.transpose(cache_logf, (0, 1, 3, 2)))
    y_s, feats_s, s_s = _path(xs, ts_, cache, state_hgrn, wts)

    k_p, v_p, lf_p = _caches_out(feats_p, sp, nh, hd)
    k_s, v_s, lf_s = _caches_out(feats_s, ts_, nh, hd)
    return (y_p, y_s[:, :ts_], k_p, v_p, lf_p, s_p, k_s, v_s, lf_s, s_s)
```

```python
import functools

import jax
import jax.numpy as jnp
import numpy as np
from jax import lax
from jax.experimental import pallas as pl
from jax.experimental.pallas import tpu as pltpu

F32 = jnp.float32
BF16 = jnp.bfloat16

NORM_EPS = 1e-6
HG_CHUNK = 64
HG_SAFE_LOG_RANGE = 60.0
LANES = 128
SUBLANES = 8
NEG_BIG = -1e30
ATT_SKIP_LOG = 110.0
ATT_NORM_SLACK = 1.02
LOG2_E = 1.4426950408889634
VMEM_LIMIT = 56 * 1024 * 1024

PROJ_TM = 512
ATT_T = 256
ATT_QBLOCKS = 16
HG_TT = 256

ST_CMAX, ST_CMIN, ST_KNORM, ST_QNORM = 0, 1, 2, 3


def _cparams(n_axes):
    return pltpu.CompilerParams(dimension_semantics=("arbitrary",) * n_axes,
                                vmem_limit_bytes=VMEM_LIMIT)


def _log_sigmoid(x):
    return jnp.minimum(x, 0.0) - jnp.log1p(jnp.exp(-jnp.abs(x)))


def _silu(x):
    return x / (1.0 + jnp.exp(-x))


def _split3(a):
    hi = a.astype(BF16)
    r = a - hi.astype(F32)
    mid = r.astype(BF16)
    lo = (r - mid.astype(F32)).astype(BF16)
    return hi, mid, lo


def _dot(a, b):
    return jnp.dot(a, b, preferred_element_type=F32)


def _dot_nt(a, b):
    return lax.dot_general(a, b, (((1,), (1,)), ((), ())), preferred_element_type=F32)


def _dot_tn(a, b):
    return lax.dot_general(a, b, (((0,), (0,)), ((), ())), preferred_element_type=F32)


def _cumulative(lft, tri, carry_ref):
    n = lft.shape[1]
    ct = carry_ref[:, 0:1]
    for part in _split3(lft):
        ct = ct + _dot(part, tri)
    carry_ref[...] = jnp.broadcast_to(ct[:, n - 1:n], carry_ref.shape)
    return ct


def _operand_tile(xT, parts, h, hd, is_query):
    n = xT.shape[1]
    r = lax.broadcasted_iota(jnp.int32, (SUBLANES, n), 0)
    hi, mid, lo = (jnp.broadcast_to(p[h:h + 1, :].astype(F32), (SUBLANES, n)) for p in parts)
    if is_query:
        rows = jnp.where(r == 0, hi, jnp.where(r == 1, mid, jnp.where(r == 2, lo,
                                                                       jnp.where(r < 6, 1.0, 0.0))))
    else:
        rows = jnp.where(r < 3, 1.0, jnp.where(r == 3, -hi, jnp.where(r == 4, -mid,
                                                                       jnp.where(r == 5, -lo, 0.0))))
    pad = jnp.zeros((LANES - hd - SUBLANES, n), F32)
    return jnp.concatenate([xT[h * hd:(h + 1) * hd, :], rows, pad], axis=0).astype(BF16)


def _sq_norms(xT, nh, hd):
    xr = xT.astype(BF16).astype(F32)
    return jnp.concatenate([jnp.sum(xr[h * hd:(h + 1) * hd, :] ** 2, axis=0, keepdims=True)
                            for h in range(nh)], axis=0)


def _block_stats(c2, k2, q2):
    lane = lax.broadcasted_iota(jnp.int32, (c2.shape[0], LANES), 1)
    top = lambda a: jnp.max(a, axis=1, keepdims=True)
    qn = jnp.zeros_like(top(k2)) if q2 is None else jnp.sqrt(top(q2))
    return jnp.where(lane == ST_CMAX, top(c2), jnp.where(lane == ST_CMIN, -top(-c2), jnp.where(
        lane == ST_KNORM, jnp.sqrt(top(k2)), jnp.where(lane == ST_QNORM, qn, 0.0))))


def _proj_body(layer, nh, hd, fused, x_ref, g_ref, wm_ref, wf_ref, bf_ref, hl_ref, *rest):
    if fused:
        tri_ref = rest[0]
        (q_ref, kT_ref, vT_ref, lfT_ref, fg_ref, hq_ref, hlf_ref, hk_ref, hi_ref, hg_ref,
         kp_ref, vTb_ref, st_ref, carry_ref) = rest[-14:]
    else:
        (q_ref, kT_ref, vT_ref, lfT_ref, fg_ref, hq_ref, hlf_ref, hk_ref, hi_ref, hg_ref) = rest[-10:]
    fw = nh * hd
    scale = float(hd) ** -0.5 * LOG2_E
    x = x_ref[...]
    inv = lax.rsqrt(jnp.mean(x * x, axis=-1, keepdims=True) + NORM_EPS)
    hb = ((x * inv) * g_ref[...]).astype(BF16)

    def col(i):
        return _dot(hb, wm_ref[:, i * fw:(i + 1) * fw])

    lf = _log_sigmoid(_dot(hb, wf_ref[...]) + bf_ref[...])
    lft = lf.T[:nh, :]
    lfT_ref[...] = lft
    qT = (col(0) * scale).T
    kT = col(1).T
    vT = col(2).T
    kT_ref[...] = kT
    vT_ref[...] = vT
    if fused:
        @pl.when(pl.program_id(1) == 0)
        def _():
            carry_ref[...] = jnp.zeros_like(carry_ref)

        vTb_ref[...] = vT.astype(BF16)
        c2 = _cumulative(lft, tri_ref[...], carry_ref) * LOG2_E
        parts = _split3(c2)
        for h in range(nh):
            q_ref[h] = _operand_tile(qT, parts, h, hd, True)
            kp_ref[h] = _operand_tile(kT, parts, h, hd, False)
        k2, q2 = _sq_norms(kT, nh, hd), _sq_norms(qT, nh, hd)
        for i in range(x.shape[0] // ATT_T):
            sl = slice(i * ATT_T, (i + 1) * ATT_T)
            st_ref[i] = _block_stats(c2[:, sl], k2[:, sl], q2[:, sl])
    else:
        q_ref[...] = qT.astype(BF16)

    fg_ref[...] = col(3).astype(BF16)
    hq_ref[...] = _silu(col(4)).astype(BF16)

    rows = [hl_ref[i:i + 1, :] for i in range(hl_ref.shape[0])]
    mx = functools.reduce(jnp.maximum, rows)
    es = [jnp.exp(r - mx) for r in rows]
    tot = functools.reduce(lambda a, b: a + b, es)
    lb = jnp.zeros_like(mx)
    for i in range(1, layer + 1):
        lb = lb + es[i] / tot

    zf = col(5)
    a = jnp.log(lb)
    b = jnp.log1p(-lb) + _log_sigmoid(zf)
    hlf_ref[...] = jnp.maximum(a, b) + jnp.log1p(jnp.exp(-jnp.abs(a - b)))
    hk_ref[...] = ((1.0 - lb) / (1.0 + jnp.exp(zf))).astype(BF16)
    hi_ref[...] = col(6).astype(BF16)
    hg_ref[...] = col(7).astype(BF16)


def _proj(x, wts, layer, prev, fused):
    b, t, d = x.shape
    nh, hd = wts["nh"], wts["hd"]
    depth = wts["hl"].shape[0]
    fw = nh * hd
    tm = min(PROJ_TM, t)
    seq = lambda bi, s: (bi, s, 0)
    posT = lambda bi, s: (bi, 0, s)
    headT = lambda bi, s: (bi, 0, 0, s)
    featT = lambda bi, s: (layer, bi, 0, s)
    c2 = lambda bi, s: (0, 0)
    g, wm, wf, bfp = wts["g"][layer], wts["wm"][layer], wts["wf"][layer], wts["bf"][layer]
    args = [x, g, wm, wf, bfp, wts["hl"]]
    if fused:
        args.append(jnp.asarray(np.triu(np.ones((tm, tm), np.float32)), BF16))
    in_specs = [pl.BlockSpec((None, tm, d), seq)] + [pl.BlockSpec(a.shape, c2) for a in args[1:]]
    aliases = {}
    if prev is not None:
        for i, a in enumerate(prev):
            in_specs.append(pl.BlockSpec(memory_space=pl.ANY))
            aliases[len(args)] = 1 + i
            args.append(a)
    tok = pl.BlockSpec((None, tm, fw), seq)
    b16o = jax.ShapeDtypeStruct((b, t, fw), BF16)
    feat = pl.BlockSpec((None, None, fw, tm), featT)
    out_specs = [pl.BlockSpec((None, fw, tm), posT), feat, feat, pl.BlockSpec((None, None, nh, tm), featT),
                 tok, tok, tok, tok, tok, tok]
    out_shape = [jax.ShapeDtypeStruct((b, fw, t), BF16), jax.ShapeDtypeStruct((depth, b, fw, t), F32),
                 jax.ShapeDtypeStruct((depth, b, fw, t), F32),
                 jax.ShapeDtypeStruct((depth, b, nh, t), F32),
                 b16o, b16o, jax.ShapeDtypeStruct((b, t, fw), F32), b16o, b16o, b16o]
    scratch = []
    if fused:
        operand = pl.BlockSpec((None, nh, LANES, tm), headT)
        operand_shape = jax.ShapeDtypeStruct((b, nh, LANES, t), BF16)
        out_specs[0], out_shape[0] = operand, operand_shape
        out_specs += [operand, pl.BlockSpec((None, fw, tm), posT),
                      pl.BlockSpec((None, tm // ATT_T, nh, LANES), lambda bi, s: (bi, s, 0, 0))]
        out_shape += [operand_shape, jax.ShapeDtypeStruct((b, fw, t), BF16),
                      jax.ShapeDtypeStruct((b, t // ATT_T, nh, LANES), F32)]
        scratch = [pltpu.VMEM((nh, LANES), F32)]
    return pl.pallas_call(
        functools.partial(_proj_body, layer, nh, hd, fused),
        grid=(b, t // tm),
        in_specs=in_specs,
        out_specs=out_specs,
        out_shape=out_shape,
        scratch_shapes=scratch,
        input_output_aliases=aliases,
        compiler_params=_cparams(2),
        name="proj",
    )(*args)


def _prep_body(nh, hd, n_past, q_ref, kc_ref, vc_ref, lfc_ref, kn_ref, vn_ref, lfn_ref, tri_ref,
               qT_ref, kp_ref, vT_ref, st_ref, carry_ref):
    is_new = pl.program_id(1) >= n_past
    kT = jnp.where(is_new, kn_ref[...], kc_ref[...])
    vT = jnp.where(is_new, vn_ref[...], vc_ref[...])
    lft = jnp.where(is_new, lfn_ref[...], lfc_ref[...])

    @pl.when(pl.program_id(1) == 0)
    def _():
        carry_ref[...] = jnp.zeros_like(carry_ref)

    c2 = _cumulative(lft, tri_ref[...], carry_ref) * LOG2_E
    parts = _split3(c2)
    for h in range(nh):
        kp_ref[h] = _operand_tile(kT, parts, h, hd, False)
    vT_ref[...] = vT.astype(BF16)
    k2 = _sq_norms(kT, nh, hd)
    st_ref[...] = _block_stats(c2, k2, None)

    @pl.when(is_new)
    def _():
        qT = q_ref[...].astype(F32)
        for h in range(nh):
            qT_ref[h] = _operand_tile(qT, parts, h, hd, True)
        st_ref[...] = _block_stats(c2, k2, _sq_norms(qT, nh, hd))


def _prep(qT, new, cache, layer, nh, hd):
    b, fw, tq = qT.shape
    ts = ATT_T
    n_past = cache[0].shape[3] // ts
    nblk = n_past + tq // ts
    tri = jnp.asarray(np.triu(np.ones((ts, ts), np.float32)), BF16)
    new_blk = lambda s: jnp.maximum(s - n_past, 0)
    featT_new = lambda bi, s: (layer, bi, 0, new_blk(s))
    featT_old = lambda bi, s: (layer, bi, 0, jnp.minimum(s, n_past - 1))

    def feat_specs(idx):
        return [pl.BlockSpec((None, None, fw, ts), idx), pl.BlockSpec((None, None, fw, ts), idx),
                pl.BlockSpec((None, None, nh, ts), idx)]

    in_specs = ([pl.BlockSpec((None, fw, ts), lambda bi, s: (bi, 0, new_blk(s)))]
                + feat_specs(featT_old) + feat_specs(featT_new)
                + [pl.BlockSpec(tri.shape, lambda bi, s: (0, 0))])
    return pl.pallas_call(
        functools.partial(_prep_body, nh, hd, n_past),
        grid=(b, nblk),
        in_specs=in_specs,
        out_specs=[pl.BlockSpec((None, nh, LANES, ts), lambda bi, s: (bi, 0, 0, new_blk(s))),
                   pl.BlockSpec((None, nh, LANES, ts), lambda bi, s: (bi, 0, 0, s)),
                   pl.BlockSpec((None, fw, ts), lambda bi, s: (bi, 0, s)),
                   pl.BlockSpec((None, None, nh, LANES), lambda bi, s: (bi, s, 0, 0))],
        out_shape=[jax.ShapeDtypeStruct((b, nh, LANES, tq), BF16),
                   jax.ShapeDtypeStruct((b, nh, LANES, nblk * ts), BF16),
                   jax.ShapeDtypeStruct((b, fw, nblk * ts), BF16),
                   jax.ShapeDtypeStruct((b, nblk, nh, LANES), F32)],
        scratch_shapes=[pltpu.VMEM((nh, LANES), F32)],
        compiler_params=_cparams(2),
        name="prep",
    )(qT, *cache, *new, tri)


def _attn_body(q_off, hd, nqb, nhs, qT_ref, kp_ref, vT_ref, st_ref, o_ref):
    t = ATT_T
    nblk = st_ref.shape[2]
    blk_id = lax.broadcasted_iota(jnp.int32, (1, nblk), 1)
    causal = (lax.broadcasted_iota(jnp.int32, (t, t), 0)
              <= lax.broadcasted_iota(jnp.int32, (t, t), 1))
    cols = lambda j, n: pl.ds(pl.multiple_of(j * t, t), n)

    def online(j, carry, qT, hh):
        m, l, acc = carry
        s = _dot_tn(kp_ref[hh, :, cols(j, t)], qT)
        m_new = jnp.maximum(m, jnp.max(s, axis=0, keepdims=True))
        alpha = jnp.exp2(m - m_new)
        p = jnp.exp2(s - m_new)
        l = alpha * l + jnp.sum(p, axis=0, keepdims=True)
        acc = alpha * acc + _dot(vT_ref[hh * hd:(hh + 1) * hd, cols(j, t)], p.astype(BF16))
        return m_new, l, acc

    heads = []
    for hh in range(nhs):
        for u in range(nqb):
            qblk = q_off + pl.program_id(1) * nqb + u
            prev = jnp.maximum(qblk - 1, 0)
            qT = qT_ref[hh, :, u * t:(u + 1) * t]
            s_prev = _dot_tn(kp_ref[hh, :, cols(prev, t)], qT)
            s_diag = _dot_tn(kp_ref[hh, :, cols(qblk, t)], qT)
            heads.append((hh, u, qblk, prev, qT, s_prev, s_diag))

    chains = []
    for hh, u, qblk, wstart, qT, s_prev, s_diag in heads:
        gone = jnp.where(qblk >= 1, 0.0, -NEG_BIG)
        s_diag = jnp.where(causal, s_diag, NEG_BIG)
        m = jnp.maximum(jnp.max(s_prev, axis=0, keepdims=True) - gone,
                        jnp.max(s_diag, axis=0, keepdims=True))
        p_prev = jnp.exp2(s_prev - (m + gone))
        p_diag = jnp.exp2(s_diag - m)
        l = jnp.sum(p_prev, axis=0, keepdims=True) + jnp.sum(p_diag, axis=0, keepdims=True)
        v_rows = slice(hh * hd, (hh + 1) * hd)
        acc = (_dot(vT_ref[v_rows, cols(wstart, t)], p_prev.astype(BF16))
               + _dot(vT_ref[v_rows, cols(qblk, t)], p_diag.astype(BF16)))

        st = st_ref[hh]
        sel = blk_id == qblk
        pick = lambda r: jnp.sum(jnp.where(sel, st[r:r + 1, :], 0.0), axis=1, keepdims=True)
        bound = (ATT_NORM_SLACK * pick(ST_QNORM) * (st[ST_KNORM:ST_KNORM + 1, :] + pick(ST_KNORM))
                 + (pick(ST_CMAX) - st[ST_CMIN:ST_CMIN + 1, :]))
        live = jnp.logical_and(bound >= -ATT_SKIP_LOG * LOG2_E, blk_id < wstart)
        first = jnp.min(jnp.where(live, blk_id, wstart))
        chains.append((hh, u, qT, wstart, first, (m, l, acc)))

    if nqb == 1:
        wstart = chains[0][3]
        first = functools.reduce(jnp.minimum, [ch[4] for ch in chains])

        def body(j, carries):
            return tuple(online(j, c, ch[2], ch[0]) for c, ch in zip(carries, chains))

        carries = lax.fori_loop(first, wstart, body, tuple(ch[5] for ch in chains))
    else:
        carries = [lax.fori_loop(first, wstart, lambda j, c, qT=qT, hh=hh: online(j, c, qT, hh), carry)
                   for hh, u, qT, wstart, first, carry in chains]
    for (hh, u, _, _, _, _), (_, l, acc) in zip(chains, carries):
        o_ref[hh * hd:(hh + 1) * hd, u * t:(u + 1) * t] = (acc / l).astype(o_ref.dtype)


def _attn(qT, kp, vT, stats):
    b, nh, _, tq_all = qT.shape
    tk_all = kp.shape[3]
    hd = vT.shape[1] // nh
    t = ATT_T
    nblk = tk_all // t
    q_off = (tk_all - tq_all) // t
    nqb = ATT_QBLOCKS if (tq_all // t) % ATT_QBLOCKS == 0 else 1
    nhs = 1 if nqb > 1 else nh
    ng = nh // nhs
    tq = nqb * t
    head_blk = lambda g, i: (g // ng, g % ng, 0, 0)
    return pl.pallas_call(
        functools.partial(_attn_body, q_off, hd, nqb, nhs),
        grid=(b * ng, tq_all // tq),
        in_specs=[pl.BlockSpec((None, nhs, LANES, tq), lambda g, i: (g // ng, g % ng, 0, i)),
                  pl.BlockSpec((None, nhs, LANES, tk_all), head_blk),
                  pl.BlockSpec((None, nhs * hd, tk_all), lambda g, i: (g // ng, g % ng, 0)),
                  pl.BlockSpec((None, nhs, SUBLANES, nblk), head_blk)],
        out_specs=pl.BlockSpec((None, nhs * hd, tq), lambda g, i: (g // ng, g % ng, i)),
        out_shape=jax.ShapeDtypeStruct((b, nh * hd, tq_all), BF16),
        compiler_params=_cparams(2),
        name="attn",
    )(qT, kp, vT, stats)


def _hgrn_body(nhg, q_ref, lf_ref, k_ref, v_ref, s0_ref, tri_ref, o_ref, sfin_ref,
               st_ref, b_ref, qf_ref, kf_ref, oi_ref, ox_ref, qb_ref, u_ref, dec_ref):
    tt, w = lf_ref.shape
    hd = w // nhg
    cl = HG_CHUNK
    nchunk = tt // cl

    @pl.when(pl.program_id(1) == 0)
    def _():
        for h in range(nhg):
            st_ref[h] = s0_ref[h].T

    tri = tri_ref[...]
    b = jnp.zeros((tt, w), F32)
    for part in _split3(lf_ref[...]):
        b = b + _dot(tri, part)
    qf = q_ref[...].astype(F32)
    kf = k_ref[...].astype(F32)
    for h in range(nhg):
        hs = slice(h * hd, (h + 1) * hd)
        b_ref[h], qf_ref[h], kf_ref[h] = b[:, hs], qf[:, hs], kf[:, hs]

    rid = lax.broadcasted_iota(jnp.int32, (cl, cl), 0)
    cid = lax.broadcasted_iota(jnp.int32, (cl, cl), 1)
    zpad = jnp.zeros((LANES - cl, hd), F32)
    worst = jnp.zeros((1, hd), F32)
    units = [(h, c) for h in range(nhg) for c in range(nchunk)]
    scores = {}
    for h, c in units:
        hs, sl = slice(h * hd, (h + 1) * hd), slice(c * cl, (c + 1) * cl)
        bc, qc, kc = b[sl, hs], qf[sl, hs], kf[sl, hs]
        vc = v_ref[sl, hs]
        r = bc[cl // 2 - 1:cl // 2, :]
        last = bc[cl - 1:cl, :]
        worst = jnp.maximum(worst, jnp.maximum(-r, r - last))
        qe = (qc * jnp.exp(bc - r)).astype(BF16)
        ke = (kc * jnp.exp(r - bc)).astype(BF16)
        scores[h, c] = _dot_nt(qe, ke)
        qb_ref[h, sl, :] = (qc * jnp.exp(bc)).astype(BF16)
        ke2 = jnp.concatenate([kc * jnp.exp(last - bc), zpad], axis=0).astype(BF16)
        vt = jnp.concatenate([vc.astype(F32), zpad], axis=0).T.astype(BF16)
        u_ref[h, c] = _dot(vt, ke2)
        dec_ref[h * nchunk + c:h * nchunk + c + 1, :] = jnp.exp(last)
    for h, c in units:
        hs, sl = slice(h * hd, (h + 1) * hd), slice(c * cl, (c + 1) * cl)
        a = jnp.where(rid >= cid, scores[h, c], 0.0)
        oi_ref[h, sl, :] = _dot(a.astype(BF16), v_ref[sl, hs])

    states = [st_ref[h] for h in range(nhg)]
    for c in range(nchunk):
        sl = slice(c * cl, (c + 1) * cl)
        for h in range(nhg):
            ox_ref[h, sl, :] = _dot_nt(qb_ref[h, sl, :], states[h].astype(BF16))
            i = h * nchunk + c
            states[h] = states[h] * dec_ref[i:i + 1, :] + u_ref[h, c]
    for h in range(nhg):
        st_ref[h] = states[h]

    def row(t, _):
        base = pl.multiple_of((t // cl) * cl, cl)
        spos = base + lax.broadcasted_iota(jnp.int32, (cl, 1), 0)
        for h in range(nhg):
            hs = slice(h * hd, (h + 1) * hd)
            bt = b_ref[h, pl.ds(t, 1), :]
            qt = qf_ref[h, pl.ds(t, 1), :]
            e = jnp.exp(jnp.minimum(bt - b_ref[h, pl.ds(base, cl), :], 0.0))
            wgt = jnp.where(spos <= t, qt * kf_ref[h, pl.ds(base, cl), :] * e, 0.0)
            a = jnp.sum(wgt, axis=1, keepdims=True)
            vch = v_ref[pl.ds(base, cl), hs].astype(F32)
            oi_ref[h, pl.ds(t, 1), :] = jnp.sum(a * vch, axis=0, keepdims=True)
        return 0

    unsafe_rows = jnp.where(jnp.max(worst) <= HG_SAFE_LOG_RANGE, 0, tt)
    lax.fori_loop(0, unsafe_rows, row, 0)
    for h in range(nhg):
        o_ref[:, h * hd:(h + 1) * hd] = (ox_ref[h] + oi_ref[h]).astype(o_ref.dtype)

    @pl.when(pl.program_id(1) == pl.num_programs(1) - 1)
    def _():
        for h in range(nhg):
            sfin_ref[h] = st_ref[h].T


def _hgrn(hq, hlf, hk, hi, s0, t_valid):
    b, _, w = hq.shape
    nhg, hd = s0.shape[1], s0.shape[2]
    tt = min(HG_TT, t_valid)
    cl = HG_CHUNK
    nchunk = tt // cl
    tri = jnp.asarray(np.kron(np.eye(nchunk, dtype=np.float32),
                              np.tril(np.ones((cl, cl), np.float32))), BF16)
    seq = lambda bi, s: (bi, s, 0)
    sidx = lambda bi, s: (bi, 0, 0, 0)
    blk = pl.BlockSpec((None, tt, w), seq)
    sblk = pl.BlockSpec((None, nhg, hd, hd), sidx)
    head_rows = pltpu.VMEM((nhg, tt, hd), F32)
    return pl.pallas_call(
        functools.partial(_hgrn_body, nhg),
        grid=(b, t_valid // tt),
        in_specs=[blk, blk, blk, blk, sblk, pl.BlockSpec(tri.shape, lambda bi, s: (0, 0))],
        out_specs=[blk, sblk],
        out_shape=[jax.ShapeDtypeStruct((b, t_valid, w), BF16),
                   jax.ShapeDtypeStruct((b, nhg, hd, hd), F32)],
        scratch_shapes=[pltpu.VMEM((nhg, hd, hd), F32), head_rows, head_rows, head_rows, head_rows,
                        head_rows, pltpu.VMEM((nhg, tt, hd), BF16),
                        pltpu.VMEM((nhg, nchunk, hd, hd), F32), pltpu.VMEM((nhg * nchunk, hd), F32)],
        compiler_params=_cparams(2),
        name="hgrn",
    )(hq, hlf, hk, hi, s0, tri)


def _merge_body(final, nhg, x_ref, oT_ref, fg_ref, ho_ref, hg_ref, gn_ref, w1_ref, w2_ref, fgain_ref,
                y_ref):
    fy = (oT_ref[...].astype(F32).T * _silu(fg_ref[...].astype(F32))).astype(BF16)
    ho = ho_ref[...].astype(F32)
    hw = ho.shape[1] // nhg
    normed = []
    for h in range(nhg):
        oh = ho[:, h * hw:(h + 1) * hw]
        inv = lax.rsqrt(jnp.mean(oh * oh, axis=-1, keepdims=True) + NORM_EPS)
        normed.append(oh * inv * gn_ref[:, h * hw:(h + 1) * hw])
    hy = (jnp.concatenate(normed, axis=1) * _silu(hg_ref[...].astype(F32))).astype(BF16)
    out = x_ref[...] + _dot(fy, w1_ref[...]) + _dot(hy, w2_ref[...])
    if final:
        inv = lax.rsqrt(jnp.mean(out * out, axis=-1, keepdims=True) + NORM_EPS)
        out = (out * inv) * fgain_ref[...]
    y_ref[...] = out


def _merge(x, oT, fg, ho, hg, gn, w1, w2, fgain, nhg, final):
    b, t, d = x.shape
    fw = fg.shape[2]
    tm = min(PROJ_TM, t)
    seq = lambda bi, s: (bi, s, 0)
    const = lambda bi, s: (0, 0)
    blk = pl.BlockSpec((None, tm, fw), seq)
    return pl.pallas_call(
        functools.partial(_merge_body, final, nhg),
        grid=(b, t // tm),
        in_specs=[pl.BlockSpec((None, tm, d), seq),
                  pl.BlockSpec((None, fw, tm), lambda bi, s: (bi, 0, s)),
                  blk, blk, blk, pl.BlockSpec(gn.shape, const),
                  pl.BlockSpec(w1.shape, const), pl.BlockSpec(w2.shape, const),
                  pl.BlockSpec(fgain.shape, const)],
        out_specs=pl.BlockSpec((None, tm, d), seq),
        out_shape=jax.ShapeDtypeStruct((b, t, d), F32),
        compiler_params=_cparams(2),
        name="merge",
    )(x, oT, fg, ho, hg, gn, w1, w2, fgain)


def _path(x, t_valid, cache, states0, wts):
    nh, hd, nhg = wts["nh"], wts["hd"], wts["nhg"]
    depth = wts["hl"].shape[0]
    t = x.shape[1]
    feats = None
    s_fin = []
    for layer in range(depth):
        if cache is None:
            qT, kT, vT, lfT, fg, hq, hlf, hk, hi, hg, kp, vTb, stats = _proj(x, wts, layer, feats, True)
            feats = (kT, vT, lfT)
        else:
            qT, kT, vT, lfT, fg, hq, hlf, hk, hi, hg = _proj(x, wts, layer, feats, False)
            feats = (kT, vT, lfT)
            qT, kp, vTb, stats = _prep(qT, feats, cache, layer, nh, hd)
        stats = jnp.transpose(stats[..., :SUBLANES], (0, 2, 3, 1))
        oT = _attn(qT, kp, vTb, stats)
        ho, s_new = _hgrn(hq, hlf, hk, hi, states0[layer], t_valid)
        if t_valid < t:
            ho = jnp.pad(ho, ((0, 0), (0, t - t_valid), (0, 0)))
        x = _merge(x, oT, fg, ho, hg, wts["gn"][layer], wts["w1"][layer], wts["w2"][layer],
                   wts["fgain"], nhg, layer == depth - 1)
        s_fin.append(s_new)
    return x, feats, jnp.stack(s_fin)


def _caches_out(feats, t_valid, nh, hd):
    kT, vT, lfT = (a[..., :t_valid] for a in feats)
    depth, b = kT.shape[:2]
    to_tok = lambda a: jnp.transpose(a.reshape(depth, b, nh, hd, t_valid), (0, 1, 4, 2, 3))
    return to_tok(kT), to_tok(vT), jnp.transpose(lfT, (0, 1, 3, 2))


def kernel(x_prompt, x_sample, cache_k, cache_v, cache_logf, state_hgrn,
           norm_g, w_in, fox_b_f, hg_lower, hg_norm_g, w_out, final_g):
    depth, d = norm_g.shape
    nh = fox_b_f.shape[1]
    fw = w_out.shape[1] // 2
    hd = fw // nh
    nhg = state_hgrn.shape[2]
    assert hd % SUBLANES == 0 and hd + SUBLANES <= LANES and nh <= SUBLANES
    o = 3 * fw
    wm = jnp.concatenate([w_in[:, :, :o], w_in[:, :, o + nh:]], axis=2).astype(BF16)
    wf = jnp.pad(w_in[:, :, o:o + nh], ((0, 0), (0, 0), (0, LANES - nh))).astype(BF16)
    wts = dict(
        nh=nh, hd=hd, nhg=nhg, wm=wm, wf=wf,
        g=norm_g.reshape(depth, 1, d),
        bf=jnp.pad(fox_b_f, ((0, 0), (0, LANES - nh))).reshape(depth, 1, LANES),
        hl=hg_lower, gn=hg_norm_g.reshape(depth, 1, -1),
        w1=w_out[:, :fw, :].astype(BF16), w2=w_out[:, fw:, :].astype(BF16),
        fgain=final_g.reshape(1, d))

    bp, sp, _ = x_prompt.shape
    assert sp % PROJ_TM == 0 and PROJ_TM % ATT_T == 0 and sp >= 2 * ATT_T
    zero_states = jnp.zeros((depth, bp) + state_hgrn.shape[2:], F32)
    y_p, feats_p, s_p = _path(x_prompt, sp, None, zero_states, wts)

    bs, ts_, _ = x_sample.shape
    past = cache_k.shape[2]
    assert past % ATT_T == 0 and ts_ % HG_CHUNK == 0
    t_pad = -(-ts_ // ATT_T) * ATT_T
    xs = jnp.pad(x_sample, ((0, 0), (0, t_pad - ts_), (0, 0)))
    to_featT = lambda a: jnp.transpose(a, (0, 1, 3, 4, 2)).reshape(depth, bs, fw, past)
    cache = (to_featT(cache_k), to_featT(cache_v), jnp.transpose(cache_logf, (0, 1, 3, 2)))
    y_s, feats_s, s_s = _path(xs, ts_, cache, state_hgrn, wts)

    k_p, v_p, lf_p = _caches_out(feats_p, sp, nh, hd)
    k_s, v_s, lf_s = _caches_out(feats_s, ts_, nh, hd)
    return (y_p, y_s[:, :ts_], k_p, v_p, lf_p, s_p, k_s, v_s, lf_s, s_s)
```

```python
import functools

import jax
import jax.numpy as jnp
import numpy as np
from jax import lax
from jax.experimental import pallas as pl
from jax.experimental.pallas import tpu as pltpu

F32 = jnp.float32
BF16 = jnp.bfloat16

NORM_EPS = 1e-6
HG_CHUNK = 64
HG_SAFE_LOG_RANGE = 60.0
LANES = 128
SUBLANES = 8
NEG_BIG = -1e30
ATT_SKIP_LOG = 110.0
ATT_NORM_SLACK = 1.02
LOG2_E = 1.4426950408889634
VMEM_LIMIT = 56 * 1024 * 1024

PROJ_TM = 512
ATT_T = 256
ATT_QBLOCKS = 16
HG_TT = 256

ST_CMAX, ST_CMIN, ST_KNORM, ST_QNORM = 0, 1, 2, 3


def _cparams(n_axes):
    return pltpu.CompilerParams(dimension_semantics=("arbitrary",) * n_axes,
                                vmem_limit_bytes=VMEM_LIMIT)


def _log_sigmoid(x):
    return jnp.minimum(x, 0.0) - jnp.log(1.0 + jnp.exp(-jnp.abs(x)))


def _silu(x):
    return x / (1.0 + jnp.exp(-x))


def _split3(a):
    hi = a.astype(BF16)
    r = a - hi.astype(F32)
    mid = r.astype(BF16)
    lo = (r - mid.astype(F32)).astype(BF16)
    return hi, mid, lo


def _dot(a, b):
    return jnp.dot(a, b, preferred_element_type=F32)


def _dot_nt(a, b):
    return lax.dot_general(a, b, (((1,), (1,)), ((), ())), preferred_element_type=F32)


def _dot_tn(a, b):
    return lax.dot_general(a, b, (((0,), (0,)), ((), ())), preferred_element_type=F32)


def _cumulative(lft, tri, carry_ref):
    n = lft.shape[1]
    ct = carry_ref[:, 0:1]
    for part in _split3(lft):
        ct = ct + _dot(part, tri)
    carry_ref[...] = jnp.broadcast_to(ct[:, n - 1:n], carry_ref.shape)
    return ct


def _operand_tile(xT, parts, h, hd, is_query):
    n = xT.shape[1]
    r = lax.broadcasted_iota(jnp.int32, (SUBLANES, n), 0)
    hi, mid, lo = (jnp.broadcast_to(p[h:h + 1, :].astype(F32), (SUBLANES, n)) for p in parts)
    if is_query:
        rows = jnp.where(r == 0, hi, jnp.where(r == 1, mid, jnp.where(r == 2, lo,
                                                                       jnp.where(r < 6, 1.0, 0.0))))
    else:
        rows = jnp.where(r < 3, 1.0, jnp.where(r == 3, -hi, jnp.where(r == 4, -mid,
                                                                       jnp.where(r == 5, -lo, 0.0))))
    pad = jnp.zeros((LANES - hd - SUBLANES, n), F32)
    return jnp.concatenate([xT[h * hd:(h + 1) * hd, :], rows, pad], axis=0).astype(BF16)


def _sq_norms(xT, nh, hd):
    xr = xT.astype(BF16).astype(F32)
    return jnp.concatenate([jnp.sum(xr[h * hd:(h + 1) * hd, :] ** 2, axis=0, keepdims=True)
                            for h in range(nh)], axis=0)


def _block_stats(c2, k2, q2):
    lane = lax.broadcasted_iota(jnp.int32, (c2.shape[0], LANES), 1)
    top = lambda a: jnp.max(a, axis=1, keepdims=True)
    qn = jnp.zeros_like(top(k2)) if q2 is None else jnp.sqrt(top(q2))
    return jnp.where(lane == ST_CMAX, top(c2), jnp.where(lane == ST_CMIN, -top(-c2), jnp.where(
        lane == ST_KNORM, jnp.sqrt(top(k2)), jnp.where(lane == ST_QNORM, qn, 0.0))))


def _proj_body(layer, nh, hd, fused, x_ref, g_ref, wm_ref, wf_ref, bf_ref, hl_ref, *rest):
    if fused:
        tri_ref = rest[0]
        (q_ref, kT_ref, vT_ref, lfT_ref, fg_ref, hq_ref, hlf_ref, hk_ref, hi_ref, hg_ref,
         kp_ref, vTb_ref, st_ref, carry_ref) = rest[-14:]
    else:
        (q_ref, kT_ref, vT_ref, lfT_ref, fg_ref, hq_ref, hlf_ref, hk_ref, hi_ref, hg_ref) = rest[-10:]
    fw = nh * hd
    scale = float(hd) ** -0.5 * LOG2_E
    x = x_ref[...]
    inv = lax.rsqrt(jnp.mean(x * x, axis=-1, keepdims=True) + NORM_EPS)
    hb = ((x * inv) * g_ref[...]).astype(BF16)

    def col(i):
        return _dot(hb, wm_ref[:, i * fw:(i + 1) * fw])

    lf = _log_sigmoid(_dot(hb, wf_ref[...]) + bf_ref[...])
    lft = lf.T[:nh, :]
    lfT_ref[...] = lft
    qT = (col(0) * scale).T
    kT = col(1).T
    vT = col(2).T
    kT_ref[...] = kT
    vT_ref[...] = vT
    if fused:
        @pl.when(pl.program_id(1) == 0)
        def _():
            carry_ref[...] = jnp.zeros_like(carry_ref)

        vTb_ref[...] = vT.astype(BF16)
        c2 = _cumulative(lft, tri_ref[...], carry_ref) * LOG2_E
        parts = _split3(c2)
        for h in range(nh):
            q_ref[h] = _operand_tile(qT, parts, h, hd, True)
            kp_ref[h] = _operand_tile(kT, parts, h, hd, False)
        k2, q2 = _sq_norms(kT, nh, hd), _sq_norms(qT, nh, hd)
        for i in range(x.shape[0] // ATT_T):
            sl = slice(i * ATT_T, (i + 1) * ATT_T)
            st_ref[i] = _block_stats(c2[:, sl], k2[:, sl], q2[:, sl])
    else:
        q_ref[...] = qT.astype(BF16)

    fg_ref[...] = col(3).astype(BF16)
    hq_ref[...] = _silu(col(4)).astype(BF16)

    rows = [hl_ref[i:i + 1, :] for i in range(hl_ref.shape[0])]
    mx = functools.reduce(jnp.maximum, rows)
    es = [jnp.exp(r - mx) for r in rows]
    tot = functools.reduce(lambda a, b: a + b, es)
    lb = jnp.zeros_like(mx)
    for i in range(1, layer + 1):
        lb = lb + es[i] / tot

    zf = col(5)
    a = jnp.log(lb)
    b = jnp.log1p(-lb) + _log_sigmoid(zf)
    hlf_ref[...] = jnp.maximum(a, b) + jnp.log(1.0 + jnp.exp(-jnp.abs(a - b)))
    hk_ref[...] = ((1.0 - lb) / (1.0 + jnp.exp(zf))).astype(BF16)
    hi_ref[...] = col(6).astype(BF16)
    hg_ref[...] = col(7).astype(BF16)


def _proj(x, wts, layer, prev, fused):
    b, t, d = x.shape
    nh, hd = wts["nh"], wts["hd"]
    depth = wts["hl"].shape[0]
    fw = nh * hd
    tm = min(PROJ_TM, t)
    seq = lambda bi, s: (bi, s, 0)
    posT = lambda bi, s: (bi, 0, s)
    headT = lambda bi, s: (bi, 0, 0, s)
    featT = lambda bi, s: (layer, bi, 0, s)
    c2 = lambda bi, s: (0, 0)
    g, wm, wf, bfp = wts["g"][layer], wts["wm"][layer], wts["wf"][layer], wts["bf"][layer]
    args = [x, g, wm, wf, bfp, wts["hl"]]
    if fused:
        args.append(jnp.asarray(np.triu(np.ones((tm, tm), np.float32)), BF16))
    in_specs = [pl.BlockSpec((None, tm, d), seq)] + [pl.BlockSpec(a.shape, c2) for a in args[1:]]
    aliases = {}
    if prev is not None:
        for i, a in enumerate(prev):
            in_specs.append(pl.BlockSpec(memory_space=pl.ANY))
            aliases[len(args)] = 1 + i
            args.append(a)
    tok = pl.BlockSpec((None, tm, fw), seq)
    b16o = jax.ShapeDtypeStruct((b, t, fw), BF16)
    feat = pl.BlockSpec((None, None, fw, tm), featT)
    out_specs = [pl.BlockSpec((None, fw, tm), posT), feat, feat, pl.BlockSpec((None, None, nh, tm), featT),
                 tok, tok, tok, tok, tok, tok]
    out_shape = [jax.ShapeDtypeStruct((b, fw, t), BF16), jax.ShapeDtypeStruct((depth, b, fw, t), F32),
                 jax.ShapeDtypeStruct((depth, b, fw, t), F32),
                 jax.ShapeDtypeStruct((depth, b, nh, t), F32),
                 b16o, b16o, jax.ShapeDtypeStruct((b, t, fw), F32), b16o, b16o, b16o]
    scratch = []
    if fused:
        operand = pl.BlockSpec((None, nh, LANES, tm), headT)
        operand_shape = jax.ShapeDtypeStruct((b, nh, LANES, t), BF16)
        out_specs[0], out_shape[0] = operand, operand_shape
        out_specs += [operand, pl.BlockSpec((None, fw, tm), posT),
                      pl.BlockSpec((None, tm // ATT_T, nh, LANES), lambda bi, s: (bi, s, 0, 0))]
        out_shape += [operand_shape, jax.ShapeDtypeStruct((b, fw, t), BF16),
                      jax.ShapeDtypeStruct((b, t // ATT_T, nh, LANES), F32)]
        scratch = [pltpu.VMEM((nh, LANES), F32)]
    return pl.pallas_call(
        functools.partial(_proj_body, layer, nh, hd, fused),
        grid=(b, t // tm),
        in_specs=in_specs,
        out_specs=out_specs,
        out_shape=out_shape,
        scratch_shapes=scratch,
        input_output_aliases=aliases,
        compiler_params=_cparams(2),
        name="proj",
    )(*args)


def _prep_body(nh, hd, n_past, q_ref, kc_ref, vc_ref, lfc_ref, kn_ref, vn_ref, lfn_ref, tri_ref,
               qT_ref, kp_ref, vT_ref, st_ref, carry_ref):
    is_new = pl.program_id(1) >= n_past
    kT = jnp.where(is_new, kn_ref[...], kc_ref[...])
    vT = jnp.where(is_new, vn_ref[...], vc_ref[...])
    lft = jnp.where(is_new, lfn_ref[...], lfc_ref[...])

    @pl.when(pl.program_id(1) == 0)
    def _():
        carry_ref[...] = jnp.zeros_like(carry_ref)

    c2 = _cumulative(lft, tri_ref[...], carry_ref) * LOG2_E
    parts = _split3(c2)
    for h in range(nh):
        kp_ref[h] = _operand_tile(kT, parts, h, hd, False)
    vT_ref[...] = vT.astype(BF16)
    k2 = _sq_norms(kT, nh, hd)
    st_ref[...] = _block_stats(c2, k2, None)

    @pl.when(is_new)
    def _():
        qT = q_ref[...].astype(F32)
        for h in range(nh):
            qT_ref[h] = _operand_tile(qT, parts, h, hd, True)
        st_ref[...] = _block_stats(c2, k2, _sq_norms(qT, nh, hd))


def _prep(qT, new, cache, layer, nh, hd):
    b, fw, tq = qT.shape
    ts = ATT_T
    n_past = cache[0].shape[3] // ts
    nblk = n_past + tq // ts
    tri = jnp.asarray(np.triu(np.ones((ts, ts), np.float32)), BF16)
    new_blk = lambda s: jnp.maximum(s - n_past, 0)
    featT_new = lambda bi, s: (layer, bi, 0, new_blk(s))
    featT_old = lambda bi, s: (layer, bi, 0, jnp.minimum(s, n_past - 1))

    def feat_specs(idx):
        return [pl.BlockSpec((None, None, fw, ts), idx), pl.BlockSpec((None, None, fw, ts), idx),
                pl.BlockSpec((None, None, nh, ts), idx)]

    in_specs = ([pl.BlockSpec((None, fw, ts), lambda bi, s: (bi, 0, new_blk(s)))]
                + feat_specs(featT_old) + feat_specs(featT_new)
                + [pl.BlockSpec(tri.shape, lambda bi, s: (0, 0))])
    return pl.pallas_call(
        functools.partial(_prep_body, nh, hd, n_past),
        grid=(b, nblk),
        in_specs=in_specs,
        out_specs=[pl.BlockSpec((None, nh, LANES, ts), lambda bi, s: (bi, 0, 0, new_blk(s))),
                   pl.BlockSpec((None, nh, LANES, ts), lambda bi, s: (bi, 0, 0, s)),
                   pl.BlockSpec((None, fw, ts), lambda bi, s: (bi, 0, s)),
                   pl.BlockSpec((None, None, nh, LANES), lambda bi, s: (bi, s, 0, 0))],
        out_shape=[jax.ShapeDtypeStruct((b, nh, LANES, tq), BF16),
                   jax.ShapeDtypeStruct((b, nh, LANES, nblk * ts), BF16),
                   jax.ShapeDtypeStruct((b, fw, nblk * ts), BF16),
                   jax.ShapeDtypeStruct((b, nblk, nh, LANES), F32)],
        scratch_shapes=[pltpu.VMEM((nh, LANES), F32)],
        compiler_params=_cparams(2),
        name="prep",
    )(qT, *cache, *new, tri)


def _attn_body(q_off, hd, nqb, nhs, qT_ref, kp_ref, vT_ref, st_ref, o_ref):
    t = ATT_T
    nblk = st_ref.shape[2]
    blk_id = lax.broadcasted_iota(jnp.int32, (1, nblk), 1)
    causal = (lax.broadcasted_iota(jnp.int32, (t, t), 0)
              <= lax.broadcasted_iota(jnp.int32, (t, t), 1))
    cols = lambda j, n: pl.ds(pl.multiple_of(j * t, t), n)

    def online(j, carry, qT, hh):
        m, l, acc = carry
        s = _dot_tn(kp_ref[hh, :, cols(j, t)], qT)
        m_new = jnp.maximum(m, jnp.max(s, axis=0, keepdims=True))
        alpha = jnp.exp2(m - m_new)
        p = jnp.exp2(s - m_new)
        l = alpha * l + jnp.sum(p, axis=0, keepdims=True)
        acc = alpha * acc + _dot(vT_ref[hh * hd:(hh + 1) * hd, cols(j, t)], p.astype(BF16))
        return m_new, l, acc

    heads = []
    for hh in range(nhs):
        for u in range(nqb):
            qblk = q_off + pl.program_id(1) * nqb + u
            prev = jnp.maximum(qblk - 1, 0)
            qT = qT_ref[hh, :, u * t:(u + 1) * t]
            s_prev = _dot_tn(kp_ref[hh, :, cols(prev, t)], qT)
            s_diag = _dot_tn(kp_ref[hh, :, cols(qblk, t)], qT)
            heads.append((hh, u, qblk, prev, qT, s_prev, s_diag))

    chains = []
    for hh, u, qblk, wstart, qT, s_prev, s_diag in heads:
        gone = jnp.where(qblk >= 1, 0.0, -NEG_BIG)
        s_diag = jnp.where(causal, s_diag, NEG_BIG)
        m = jnp.maximum(jnp.max(s_prev, axis=0, keepdims=True) - gone,
                        jnp.max(s_diag, axis=0, keepdims=True))
        p_prev = jnp.exp2(s_prev - (m + gone))
        p_diag = jnp.exp2(s_diag - m)
        l = jnp.sum(p_prev, axis=0, keepdims=True) + jnp.sum(p_diag, axis=0, keepdims=True)
        v_rows = slice(hh * hd, (hh + 1) * hd)
        acc = (_dot(vT_ref[v_rows, cols(wstart, t)], p_prev.astype(BF16))
               + _dot(vT_ref[v_rows, cols(qblk, t)], p_diag.astype(BF16)))

        st = st_ref[hh]
        sel = blk_id == qblk
        pick = lambda r: jnp.sum(jnp.where(sel, st[r:r + 1, :], 0.0), axis=1, keepdims=True)
        bound = (ATT_NORM_SLACK * pick(ST_QNORM) * (st[ST_KNORM:ST_KNORM + 1, :] + pick(ST_KNORM))
                 + (pick(ST_CMAX) - st[ST_CMIN:ST_CMIN + 1, :]))
        live = jnp.logical_and(bound >= -ATT_SKIP_LOG * LOG2_E, blk_id < wstart)
        first = jnp.min(jnp.where(live, blk_id, wstart))
        chains.append((hh, u, qT, wstart, first, (m, l, acc)))

    if nqb == 1:
        wstart = chains[0][3]
        first = functools.reduce(jnp.minimum, [ch[4] for ch in chains])

        def body(j, carries):
            return tuple(online(j, c, ch[2], ch[0]) for c, ch in zip(carries, chains))

        carries = lax.fori_loop(first, wstart, body, tuple(ch[5] for ch in chains))
    else:
        carries = [lax.fori_loop(first, wstart, lambda j, c, qT=qT, hh=hh: online(j, c, qT, hh), carry)
                   for hh, u, qT, wstart, first, carry in chains]
    for (hh, u, _, _, _, _), (_, l, acc) in zip(chains, carries):
        o_ref[hh * hd:(hh + 1) * hd, u * t:(u + 1) * t] = (acc / l).astype(o_ref.dtype)


def _attn(qT, kp, vT, stats):
    b, nh, _, tq_all = qT.shape
    tk_all = kp.shape[3]
    hd = vT.shape[1] // nh
    t = ATT_T
    nblk = tk_all // t
    q_off = (tk_all - tq_all) // t
    nqb = ATT_QBLOCKS if (tq_all // t) % ATT_QBLOCKS == 0 else 1
    nhs = 1 if nqb > 1 else nh
    ng = nh // nhs
    tq = nqb * t
    head_blk = lambda g, i: (g // ng, g % ng, 0, 0)
    return pl.pallas_call(
        functools.partial(_attn_body, q_off, hd, nqb, nhs),
        grid=(b * ng, tq_all // tq),
        in_specs=[pl.BlockSpec((None, nhs, LANES, tq), lambda g, i: (g // ng, g % ng, 0, i)),
                  pl.BlockSpec((None, nhs, LANES, tk_all), head_blk),
                  pl.BlockSpec((None, nhs * hd, tk_all), lambda g, i: (g // ng, g % ng, 0)),
                  pl.BlockSpec((None, nhs, SUBLANES, nblk), head_blk)],
        out_specs=pl.BlockSpec((None, nhs * hd, tq), lambda g, i: (g // ng, g % ng, i)),
        out_shape=jax.ShapeDtypeStruct((b, nh * hd, tq_all), BF16),
        compiler_params=_cparams(2),
        name="attn",
    )(qT, kp, vT, stats)


def _hgrn_body(nhg, q_ref, lf_ref, k_ref, v_ref, s0_ref, tri_ref, o_ref, sfin_ref,
               st_ref, b_ref, qf_ref, kf_ref, oi_ref, ox_ref, qb_ref, u_ref, dec_ref):
    tt, w = lf_ref.shape
    hd = w // nhg
    cl = HG_CHUNK
    nchunk = tt // cl

    @pl.when(pl.program_id(1) == 0)
    def _():
        for h in range(nhg):
            st_ref[h] = s0_ref[h].T

    tri = tri_ref[...]
    b = jnp.zeros((tt, w), F32)
    for part in _split3(lf_ref[...]):
        b = b + _dot(tri, part)
    qf = q_ref[...].astype(F32)
    kf = k_ref[...].astype(F32)
    for h in range(nhg):
        hs = slice(h * hd, (h + 1) * hd)
        b_ref[h], qf_ref[h], kf_ref[h] = b[:, hs], qf[:, hs], kf[:, hs]

    rid = lax.broadcasted_iota(jnp.int32, (cl, cl), 0)
    cid = lax.broadcasted_iota(jnp.int32, (cl, cl), 1)
    zpad = jnp.zeros((LANES - cl, hd), F32)
    worst = jnp.zeros((1, hd), F32)
    units = [(h, c) for h in range(nhg) for c in range(nchunk)]
    scores = {}
    for h, c in units:
        hs, sl = slice(h * hd, (h + 1) * hd), slice(c * cl, (c + 1) * cl)
        bc, qc, kc = b[sl, hs], qf[sl, hs], kf[sl, hs]
        vc = v_ref[sl, hs]
        r = bc[cl // 2 - 1:cl // 2, :]
        last = bc[cl - 1:cl, :]
        worst = jnp.maximum(worst, jnp.maximum(-r, r - last))
        qe = (qc * jnp.exp(bc - r)).astype(BF16)
        ke = (kc * jnp.exp(r - bc)).astype(BF16)
        scores[h, c] = _dot_nt(qe, ke)
        qb_ref[h, sl, :] = (qc * jnp.exp(bc)).astype(BF16)
        ke2 = jnp.concatenate([kc * jnp.exp(last - bc), zpad], axis=0).astype(BF16)
        vt = jnp.concatenate([vc.astype(F32), zpad], axis=0).T.astype(BF16)
        u_ref[h, c] = _dot(vt, ke2)
        dec_ref[h * nchunk + c:h * nchunk + c + 1, :] = jnp.exp(last)
    for h, c in units:
        hs, sl = slice(h * hd, (h + 1) * hd), slice(c * cl, (c + 1) * cl)
        a = jnp.where(rid >= cid, scores[h, c], 0.0)
        oi_ref[h, sl, :] = _dot(a.astype(BF16), v_ref[sl, hs])

    states = [st_ref[h] for h in range(nhg)]
    for c in range(nchunk):
        sl = slice(c * cl, (c + 1) * cl)
        for h in range(nhg):
            ox_ref[h, sl, :] = _dot_nt(qb_ref[h, sl, :], states[h].astype(BF16))
            i = h * nchunk + c
            states[h] = states[h] * dec_ref[i:i + 1, :] + u_ref[h, c]
    for h in range(nhg):
        st_ref[h] = states[h]

    def row(t, _):
        base = pl.multiple_of((t // cl) * cl, cl)
        spos = base + lax.broadcasted_iota(jnp.int32, (cl, 1), 0)
        for h in range(nhg):
            hs = slice(h * hd, (h + 1) * hd)
            bt = b_ref[h, pl.ds(t, 1), :]
            qt = qf_ref[h, pl.ds(t, 1), :]
            e = jnp.exp(jnp.minimum(bt - b_ref[h, pl.ds(base, cl), :], 0.0))
            wgt = jnp.where(spos <= t, qt * kf_ref[h, pl.ds(base, cl), :] * e, 0.0)
            a = jnp.sum(wgt, axis=1, keepdims=True)
            vch = v_ref[pl.ds(base, cl), hs].astype(F32)
            oi_ref[h, pl.ds(t, 1), :] = jnp.sum(a * vch, axis=0, keepdims=True)
        return 0

    unsafe_rows = jnp.where(jnp.max(worst) <= HG_SAFE_LOG_RANGE, 0, tt)
    lax.fori_loop(0, unsafe_rows, row, 0)
    for h in range(nhg):
        o_ref[:, h * hd:(h + 1) * hd] = (ox_ref[h] + oi_ref[h]).astype(o_ref.dtype)

    @pl.when(pl.program_id(1) == pl.num_programs(1) - 1)
    def _():
        for h in range(nhg):
            sfin_ref[h] = st_ref[h].T


def _hgrn(hq, hlf, hk, hi, s0, t_valid):
    b, _, w = hq.shape
    nhg, hd = s0.shape[1], s0.shape[2]
    tt = min(HG_TT, t_valid)
    cl = HG_CHUNK
    nchunk = tt // cl
    tri = jnp.asarray(np.kron(np.eye(nchunk, dtype=np.float32),
                              np.tril(np.ones((cl, cl), np.float32))), BF16)
    seq = lambda bi, s: (bi, s, 0)
    sidx = lambda bi, s: (bi, 0, 0, 0)
    blk = pl.BlockSpec((None, tt, w), seq)
    sblk = pl.BlockSpec((None, nhg, hd, hd), sidx)
    head_rows = pltpu.VMEM((nhg, tt, hd), F32)
    return pl.pallas_call(
        functools.partial(_hgrn_body, nhg),
        grid=(b, t_valid // tt),
        in_specs=[blk, blk, blk, blk, sblk, pl.BlockSpec(tri.shape, lambda bi, s: (0, 0))],
        out_specs=[blk, sblk],
        out_shape=[jax.ShapeDtypeStruct((b, t_valid, w), BF16),
                   jax.ShapeDtypeStruct((b, nhg, hd, hd), F32)],
        scratch_shapes=[pltpu.VMEM((nhg, hd, hd), F32), head_rows, head_rows, head_rows, head_rows,
                        head_rows, pltpu.VMEM((nhg, tt, hd), BF16),
                        pltpu.VMEM((nhg, nchunk, hd, hd), F32), pltpu.VMEM((nhg * nchunk, hd), F32)],
        compiler_params=_cparams(2),
        name="hgrn",
    )(hq, hlf, hk, hi, s0, tri)


def _merge_body(final, nhg, x_ref, oT_ref, fg_ref, ho_ref, hg_ref, gn_ref, w1_ref, w2_ref, fgain_ref,
                y_ref):
    fy = (oT_ref[...].astype(F32).T * _silu(fg_ref[...].astype(F32))).astype(BF16)
    ho = ho_ref[...].astype(F32)
    hw = ho.shape[1] // nhg
    normed = []
    for h in range(nhg):
        oh = ho[:, h * hw:(h + 1) * hw]
        inv = lax.rsqrt(jnp.mean(oh * oh, axis=-1, keepdims=True) + NORM_EPS)
        normed.append(oh * inv * gn_ref[:, h * hw:(h + 1) * hw])
    hy = (jnp.concatenate(normed, axis=1) * _silu(hg_ref[...].astype(F32))).astype(BF16)
    out = x_ref[...] + _dot(fy, w1_ref[...]) + _dot(hy, w2_ref[...])
    if final:
        inv = lax.rsqrt(jnp.mean(out * out, axis=-1, keepdims=True) + NORM_EPS)
        out = (out * inv) * fgain_ref[...]
    y_ref[...] = out


def _merge(x, oT, fg, ho, hg, gn, w1, w2, fgain, nhg, final):
    b, t, d = x.shape
    fw = fg.shape[2]
    tm = min(PROJ_TM, t)
    seq = lambda bi, s: (bi, s, 0)
    const = lambda bi, s: (0, 0)
    blk = pl.BlockSpec((None, tm, fw), seq)
    return pl.pallas_call(
        functools.partial(_merge_body, final, nhg),
        grid=(b, t // tm),
        in_specs=[pl.BlockSpec((None, tm, d), seq),
                  pl.BlockSpec((None, fw, tm), lambda bi, s: (bi, 0, s)),
                  blk, blk, blk, pl.BlockSpec(gn.shape, const),
                  pl.BlockSpec(w1.shape, const), pl.BlockSpec(w2.shape, const),
                  pl.BlockSpec(fgain.shape, const)],
        out_specs=pl.BlockSpec((None, tm, d), seq),
        out_shape=jax.ShapeDtypeStruct((b, t, d), F32),
        compiler_params=_cparams(2),
        name="merge",
    )(x, oT, fg, ho, hg, gn, w1, w2, fgain)


def _path(x, t_valid, cache, states0, wts):
    nh, hd, nhg = wts["nh"], wts["hd"], wts["nhg"]
    depth = wts["hl"].shape[0]
    t = x.shape[1]
    feats = None
    s_fin = []
    for layer in range(depth):
        if cache is None:
            qT, kT, vT, lfT, fg, hq, hlf, hk, hi, hg, kp, vTb, stats = _proj(x, wts, layer, feats, True)
            feats = (kT, vT, lfT)
        else:
            qT, kT, vT, lfT, fg, hq, hlf, hk, hi, hg = _proj(x, wts, layer, feats, False)
            feats = (kT, vT, lfT)
            qT, kp, vTb, stats = _prep(qT, feats, cache, layer, nh, hd)
        stats = jnp.transpose(stats[..., :SUBLANES], (0, 2, 3, 1))
        oT = _attn(qT, kp, vTb, stats)
        ho, s_new = _hgrn(hq, hlf, hk, hi, states0[layer], t_valid)
        if t_valid < t:
            ho = jnp.pad(ho, ((0, 0), (0, t - t_valid), (0, 0)))
        x = _merge(x, oT, fg, ho, hg, wts["gn"][layer], wts["w1"][layer], wts["w2"][layer],
                   wts["fgain"], nhg, layer == depth - 1)
        s_fin.append(s_new)
    return x, feats, jnp.stack(s_fin)


def _caches_out(feats, t_valid, nh, hd):
    kT, vT, lfT = (a[..., :t_valid] for a in feats)
    depth, b = kT.shape[:2]
    to_tok = lambda a: jnp.transpose(a.reshape(depth, b, nh, hd, t_valid), (0, 1, 4, 2, 3))
    return to_tok(kT), to_tok(vT), jnp.transpose(lfT, (0, 1, 3, 2))


def kernel(x_prompt, x_sample, cache_k, cache_v, cache_logf, state_hgrn,
           norm_g, w_in, fox_b_f, hg_lower, hg_norm_g, w_out, final_g):
    depth, d = norm_g.shape
    nh = fox_b_f.shape[1]
    fw = w_out.shape[1] // 2
    hd = fw // nh
    nhg = state_hgrn.shape[2]
    assert hd % SUBLANES == 0 and hd + SUBLANES <= LANES and nh <= SUBLANES
    o = 3 * fw
    wm = jnp.concatenate([w_in[:, :, :o], w_in[:, :, o + nh:]], axis=2).astype(BF16)
    wf = jnp.pad(w_in[:, :, o:o + nh], ((0, 0), (0, 0), (0, LANES - nh))).astype(BF16)
    wts = dict(
        nh=nh, hd=hd, nhg=nhg, wm=wm, wf=wf,
        g=norm_g.reshape(depth, 1, d),
        bf=jnp.pad(fox_b_f, ((0, 0), (0, LANES - nh))).reshape(depth, 1, LANES),
        hl=hg_lower, gn=hg_norm_g.reshape(depth, 1, -1),
        w1=w_out[:, :fw, :].astype(BF16), w2=w_out[:, fw:, :].astype(BF16),
        fgain=final_g.reshape(1, d))

    bp, sp, _ = x_prompt.shape
    assert sp % PROJ_TM == 0 and PROJ_TM % ATT_T == 0 and sp >= 2 * ATT_T
    zero_states = jnp.zeros((depth, bp) + state_hgrn.shape[2:], F32)
    y_p, feats_p, s_p = _path(x_prompt, sp, None, zero_states, wts)

    bs, ts_, _ = x_sample.shape
    past = cache_k.shape[2]
    assert past % ATT_T == 0 and ts_ % HG_CHUNK == 0
    t_pad = -(-ts_ // ATT_T) * ATT_T
    xs = jnp.pad(x_sample, ((0, 0), (0, t_pad - ts_), (0, 0)))
    to_featT = lambda a: jnp.transpose(a, (0, 1, 3, 4, 2)).reshape(depth, bs, fw, past)
    cache = (to_featT(cache_k), to_featT(cache_v), jnp.transpose(cache_logf, (0, 1, 3, 2)))
    y_s, feats_s, s_s = _path(xs, ts_, cache, state_hgrn, wts)

    k_p, v_p, lf_p = _caches_out(feats_p, sp, nh, hd)
    k_s, v_s, lf_s = _caches_out(feats_s, ts_, nh, hd)
    return (y_p, y_s[:, :ts_], k_p, v_p, lf_p, s_p, k_s, v_s, lf_s, s_s)
```

```python
import functools

import jax
import jax.numpy as jnp
import numpy as np
from jax import lax
from jax.experimental import pallas as pl
from jax.experimental.pallas import tpu as pltpu

F32 = jnp.float32
BF16 = jnp.bfloat16

NORM_EPS = 1e-6
HG_CHUNK = 64
HG_SAFE_LOG_RANGE = 60.0
LANES = 128
SUBLANES = 8
NEG_BIG = -1e30
ATT_SKIP_LOG = 110.0
ATT_NORM_SLACK = 1.02
LOG2_E = 1.4426950408889634
VMEM_LIMIT = 56 * 1024 * 1024

PROJ_TM = 512
MERGE_TM = 1024
ATT_T = 256
ATT_QBLOCKS = 16
HG_TT = 256

ST_CMAX, ST_CMIN, ST_KNORM, ST_QNORM = 0, 1, 2, 3


def _cparams(n_axes):
    return pltpu.CompilerParams(dimension_semantics=("arbitrary",) * n_axes,
                                vmem_limit_bytes=VMEM_LIMIT)


def _log_sigmoid(x):
    return jnp.minimum(x, 0.0) - jnp.log(1.0 + jnp.exp(-jnp.abs(x)))


def _silu(x):
    return x / (1.0 + jnp.exp(-x))


def _split3(a):
    hi = a.astype(BF16)
    r = a - hi.astype(F32)
    mid = r.astype(BF16)
    lo = (r - mid.astype(F32)).astype(BF16)
    return hi, mid, lo


def _dot(a, b):
    return jnp.dot(a, b, preferred_element_type=F32)


def _dot_nt(a, b):
    return lax.dot_general(a, b, (((1,), (1,)), ((), ())), preferred_element_type=F32)


def _dot_tn(a, b):
    return lax.dot_general(a, b, (((0,), (0,)), ((), ())), preferred_element_type=F32)


def _cumulative(lft, tri, carry_ref):
    n = lft.shape[1]
    ct = carry_ref[:, 0:1]
    for part in _split3(lft):
        ct = ct + _dot(part, tri)
    carry_ref[...] = jnp.broadcast_to(ct[:, n - 1:n], carry_ref.shape)
    return ct


def _operand_tile(xT, parts, h, hd, is_query):
    n = xT.shape[1]
    r = lax.broadcasted_iota(jnp.int32, (SUBLANES, n), 0)
    hi, mid, lo = (jnp.broadcast_to(p[h:h + 1, :].astype(F32), (SUBLANES, n)) for p in parts)
    if is_query:
        rows = jnp.where(r == 0, hi, jnp.where(r == 1, mid, jnp.where(r == 2, lo,
                                                                       jnp.where(r < 6, 1.0, 0.0))))
    else:
        rows = jnp.where(r < 3, 1.0, jnp.where(r == 3, -hi, jnp.where(r == 4, -mid,
                                                                       jnp.where(r == 5, -lo, 0.0))))
    pad = jnp.zeros((LANES - hd - SUBLANES, n), F32)
    return jnp.concatenate([xT[h * hd:(h + 1) * hd, :], rows, pad], axis=0).astype(BF16)


def _sq_norms(xT, nh, hd):
    xr = xT.astype(BF16).astype(F32)
    return jnp.concatenate([jnp.sum(xr[h * hd:(h + 1) * hd, :] ** 2, axis=0, keepdims=True)
                            for h in range(nh)], axis=0)


def _block_stats(c2, k2, q2):
    lane = lax.broadcasted_iota(jnp.int32, (c2.shape[0], LANES), 1)
    top = lambda a: jnp.max(a, axis=1, keepdims=True)
    qn = jnp.zeros_like(top(k2)) if q2 is None else jnp.sqrt(top(q2))
    return jnp.where(lane == ST_CMAX, top(c2), jnp.where(lane == ST_CMIN, -top(-c2), jnp.where(
        lane == ST_KNORM, jnp.sqrt(top(k2)), jnp.where(lane == ST_QNORM, qn, 0.0))))


def _proj_body(layer, nh, hd, fused, x_ref, g_ref, wm_ref, wf_ref, bf_ref, hl_ref, *rest):
    if fused:
        tri_ref = rest[0]
        (q_ref, kT_ref, vT_ref, lfT_ref, fg_ref, hq_ref, hlf_ref, hk_ref, hi_ref, hg_ref,
         kp_ref, vTb_ref, st_ref, carry_ref) = rest[-14:]
    else:
        (q_ref, kT_ref, vT_ref, lfT_ref, fg_ref, hq_ref, hlf_ref, hk_ref, hi_ref, hg_ref) = rest[-10:]
    fw = nh * hd
    scale = float(hd) ** -0.5 * LOG2_E
    x = x_ref[...]
    inv = lax.rsqrt(jnp.mean(x * x, axis=-1, keepdims=True) + NORM_EPS)
    hb = ((x * inv) * g_ref[...]).astype(BF16)

    def col(i):
        return _dot(hb, wm_ref[:, i * fw:(i + 1) * fw])

    lf = _log_sigmoid(_dot(hb, wf_ref[...]) + bf_ref[...])
    lft = lf.T[:nh, :]
    lfT_ref[...] = lft
    qT = (col(0) * scale).T
    kT = col(1).T
    vT = col(2).T
    kT_ref[...] = kT
    vT_ref[...] = vT
    if fused:
        @pl.when(pl.program_id(1) == 0)
        def _():
            carry_ref[...] = jnp.zeros_like(carry_ref)

        vTb_ref[...] = vT.astype(BF16)
        c2 = _cumulative(lft, tri_ref[...], carry_ref) * LOG2_E
        parts = _split3(c2)
        for h in range(nh):
            q_ref[h] = _operand_tile(qT, parts, h, hd, True)
            kp_ref[h] = _operand_tile(kT, parts, h, hd, False)
        k2, q2 = _sq_norms(kT, nh, hd), _sq_norms(qT, nh, hd)
        for i in range(x.shape[0] // ATT_T):
            sl = slice(i * ATT_T, (i + 1) * ATT_T)
            st_ref[i] = _block_stats(c2[:, sl], k2[:, sl], q2[:, sl])
    else:
        q_ref[...] = qT.astype(BF16)

    fg_ref[...] = col(3).astype(BF16)
    hq_ref[...] = _silu(col(4)).astype(BF16)

    rows = [hl_ref[i:i + 1, :] for i in range(hl_ref.shape[0])]
    mx = functools.reduce(jnp.maximum, rows)
    es = [jnp.exp(r - mx) for r in rows]
    tot = functools.reduce(lambda a, b: a + b, es)
    lb = jnp.zeros_like(mx)
    for i in range(1, layer + 1):
        lb = lb + es[i] / tot

    zf = col(5)
    a = jnp.log(lb)
    b = jnp.log1p(-lb) + _log_sigmoid(zf)
    hlf_ref[...] = jnp.maximum(a, b) + jnp.log(1.0 + jnp.exp(-jnp.abs(a - b)))
    hk_ref[...] = ((1.0 - lb) / (1.0 + jnp.exp(zf))).astype(BF16)
    hi_ref[...] = col(6).astype(BF16)
    hg_ref[...] = col(7).astype(BF16)


def _proj(x, wts, layer, prev, fused):
    b, t, d = x.shape
    nh, hd = wts["nh"], wts["hd"]
    depth = wts["hl"].shape[0]
    fw = nh * hd
    tm = min(PROJ_TM, t)
    seq = lambda bi, s: (bi, s, 0)
    posT = lambda bi, s: (bi, 0, s)
    headT = lambda bi, s: (bi, 0, 0, s)
    featT = lambda bi, s: (layer, bi, 0, s)
    c2 = lambda bi, s: (0, 0)
    g, wm, wf, bfp = wts["g"][layer], wts["wm"][layer], wts["wf"][layer], wts["bf"][layer]
    args = [x, g, wm, wf, bfp, wts["hl"]]
    if fused:
        args.append(jnp.asarray(np.triu(np.ones((tm, tm), np.float32)), BF16))
    in_specs = [pl.BlockSpec((None, tm, d), seq)] + [pl.BlockSpec(a.shape, c2) for a in args[1:]]
    aliases = {}
    if prev is not None:
        for i, a in enumerate(prev):
            in_specs.append(pl.BlockSpec(memory_space=pl.ANY))
            aliases[len(args)] = 1 + i
            args.append(a)
    tok = pl.BlockSpec((None, tm, fw), seq)
    b16o = jax.ShapeDtypeStruct((b, t, fw), BF16)
    feat = pl.BlockSpec((None, None, fw, tm), featT)
    out_specs = [pl.BlockSpec((None, fw, tm), posT), feat, feat, pl.BlockSpec((None, None, nh, tm), featT),
                 tok, tok, tok, tok, tok, tok]
    out_shape = [jax.ShapeDtypeStruct((b, fw, t), BF16), jax.ShapeDtypeStruct((depth, b, fw, t), F32),
                 jax.ShapeDtypeStruct((depth, b, fw, t), F32),
                 jax.ShapeDtypeStruct((depth, b, nh, t), F32),
                 b16o, b16o, jax.ShapeDtypeStruct((b, t, fw), F32), b16o, b16o, b16o]
    scratch = []
    if fused:
        operand = pl.BlockSpec((None, nh, LANES, tm), headT)
        operand_shape = jax.ShapeDtypeStruct((b, nh, LANES, t), BF16)
        out_specs[0], out_shape[0] = operand, operand_shape
        out_specs += [operand, pl.BlockSpec((None, fw, tm), posT),
                      pl.BlockSpec((None, tm // ATT_T, nh, LANES), lambda bi, s: (bi, s, 0, 0))]
        out_shape += [operand_shape, jax.ShapeDtypeStruct((b, fw, t), BF16),
                      jax.ShapeDtypeStruct((b, t // ATT_T, nh, LANES), F32)]
        scratch = [pltpu.VMEM((nh, LANES), F32)]
    return pl.pallas_call(
        functools.partial(_proj_body, layer, nh, hd, fused),
        grid=(b, t // tm),
        in_specs=in_specs,
        out_specs=out_specs,
        out_shape=out_shape,
        scratch_shapes=scratch,
        input_output_aliases=aliases,
        compiler_params=_cparams(2),
        name="proj",
    )(*args)


def _prep_body(nh, hd, n_past, q_ref, kc_ref, vc_ref, lfc_ref, kn_ref, vn_ref, lfn_ref, tri_ref,
               qT_ref, kp_ref, vT_ref, st_ref, carry_ref):
    is_new = pl.program_id(1) >= n_past
    kT = jnp.where(is_new, kn_ref[...], kc_ref[...])
    vT = jnp.where(is_new, vn_ref[...], vc_ref[...])
    lft = jnp.where(is_new, lfn_ref[...], lfc_ref[...])

    @pl.when(pl.program_id(1) == 0)
    def _():
        carry_ref[...] = jnp.zeros_like(carry_ref)

    c2 = _cumulative(lft, tri_ref[...], carry_ref) * LOG2_E
    parts = _split3(c2)
    for h in range(nh):
        kp_ref[h] = _operand_tile(kT, parts, h, hd, False)
    vT_ref[...] = vT.astype(BF16)
    k2 = _sq_norms(kT, nh, hd)
    st_ref[...] = _block_stats(c2, k2, None)

    @pl.when(is_new)
    def _():
        qT = q_ref[...].astype(F32)
        for h in range(nh):
            qT_ref[h] = _operand_tile(qT, parts, h, hd, True)
        st_ref[...] = _block_stats(c2, k2, _sq_norms(qT, nh, hd))


def _prep(qT, new, cache, layer, nh, hd):
    b, fw, tq = qT.shape
    ts = ATT_T
    n_past = cache[0].shape[3] // ts
    nblk = n_past + tq // ts
    tri = jnp.asarray(np.triu(np.ones((ts, ts), np.float32)), BF16)
    new_blk = lambda s: jnp.maximum(s - n_past, 0)
    featT_new = lambda bi, s: (layer, bi, 0, new_blk(s))
    featT_old = lambda bi, s: (layer, bi, 0, jnp.minimum(s, n_past - 1))

    def feat_specs(idx):
        return [pl.BlockSpec((None, None, fw, ts), idx), pl.BlockSpec((None, None, fw, ts), idx),
                pl.BlockSpec((None, None, nh, ts), idx)]

    in_specs = ([pl.BlockSpec((None, fw, ts), lambda bi, s: (bi, 0, new_blk(s)))]
                + feat_specs(featT_old) + feat_specs(featT_new)
                + [pl.BlockSpec(tri.shape, lambda bi, s: (0, 0))])
    return pl.pallas_call(
        functools.partial(_prep_body, nh, hd, n_past),
        grid=(b, nblk),
        in_specs=in_specs,
        out_specs=[pl.BlockSpec((None, nh, LANES, ts), lambda bi, s: (bi, 0, 0, new_blk(s))),
                   pl.BlockSpec((None, nh, LANES, ts), lambda bi, s: (bi, 0, 0, s)),
                   pl.BlockSpec((None, fw, ts), lambda bi, s: (bi, 0, s)),
                   pl.BlockSpec((None, None, nh, LANES), lambda bi, s: (bi, s, 0, 0))],
        out_shape=[jax.ShapeDtypeStruct((b, nh, LANES, tq), BF16),
                   jax.ShapeDtypeStruct((b, nh, LANES, nblk * ts), BF16),
                   jax.ShapeDtypeStruct((b, fw, nblk * ts), BF16),
                   jax.ShapeDtypeStruct((b, nblk, nh, LANES), F32)],
        scratch_shapes=[pltpu.VMEM((nh, LANES), F32)],
        compiler_params=_cparams(2),
        name="prep",
    )(qT, *cache, *new, tri)


def _attn_body(q_off, hd, nqb, nhs, qT_ref, kp_ref, vT_ref, st_ref, o_ref):
    t = ATT_T
    nblk = st_ref.shape[2]
    blk_id = lax.broadcasted_iota(jnp.int32, (1, nblk), 1)
    causal = (lax.broadcasted_iota(jnp.int32, (t, t), 0)
              <= lax.broadcasted_iota(jnp.int32, (t, t), 1))
    cols = lambda j, n: pl.ds(pl.multiple_of(j * t, t), n)

    def online(j, carry, qT, hh):
        m, l, acc = carry
        s = _dot_tn(kp_ref[hh, :, cols(j, t)], qT)
        m_new = jnp.maximum(m, jnp.max(s, axis=0, keepdims=True))
        alpha = jnp.exp2(m - m_new)
        p = jnp.exp2(s - m_new)
        l = alpha * l + jnp.sum(p, axis=0, keepdims=True)
        acc = alpha * acc + _dot(vT_ref[hh * hd:(hh + 1) * hd, cols(j, t)], p.astype(BF16))
        return m_new, l, acc

    heads = []
    for hh in range(nhs):
        for u in range(nqb):
            qblk = q_off + pl.program_id(1) * nqb + u
            prev = jnp.maximum(qblk - 1, 0)
            qT = qT_ref[hh, :, u * t:(u + 1) * t]
            s_prev = _dot_tn(kp_ref[hh, :, cols(prev, t)], qT)
            s_diag = _dot_tn(kp_ref[hh, :, cols(qblk, t)], qT)
            heads.append((hh, u, qblk, prev, qT, s_prev, s_diag))

    chains = []
    for hh, u, qblk, wstart, qT, s_prev, s_diag in heads:
        gone = jnp.where(qblk >= 1, 0.0, -NEG_BIG)
        s_diag = jnp.where(causal, s_diag, NEG_BIG)
        m = jnp.maximum(jnp.max(s_prev, axis=0, keepdims=True) - gone,
                        jnp.max(s_diag, axis=0, keepdims=True))
        p_prev = jnp.exp2(s_prev - (m + gone))
        p_diag = jnp.exp2(s_diag - m)
        l = jnp.sum(p_prev, axis=0, keepdims=True) + jnp.sum(p_diag, axis=0, keepdims=True)
        v_rows = slice(hh * hd, (hh + 1) * hd)
        acc = (_dot(vT_ref[v_rows, cols(wstart, t)], p_prev.astype(BF16))
               + _dot(vT_ref[v_rows, cols(qblk, t)], p_diag.astype(BF16)))

        st = st_ref[hh]
        sel = blk_id == qblk
        pick = lambda r: jnp.sum(jnp.where(sel, st[r:r + 1, :], 0.0), axis=1, keepdims=True)
        bound = (ATT_NORM_SLACK * pick(ST_QNORM) * (st[ST_KNORM:ST_KNORM + 1, :] + pick(ST_KNORM))
                 + (pick(ST_CMAX) - st[ST_CMIN:ST_CMIN + 1, :]))
        live = jnp.logical_and(bound >= -ATT_SKIP_LOG * LOG2_E, blk_id < wstart)
        first = jnp.min(jnp.where(live, blk_id, wstart))
        chains.append((hh, u, qT, wstart, first, (m, l, acc)))

    if nqb == 1:
        wstart = chains[0][3]
        first = functools.reduce(jnp.minimum, [ch[4] for ch in chains])

        def body(j, carries):
            return tuple(online(j, c, ch[2], ch[0]) for c, ch in zip(carries, chains))

        carries = lax.fori_loop(first, wstart, body, tuple(ch[5] for ch in chains))
    else:
        carries = [lax.fori_loop(first, wstart, lambda j, c, qT=qT, hh=hh: online(j, c, qT, hh), carry)
                   for hh, u, qT, wstart, first, carry in chains]
    for (hh, u, _, _, _, _), (_, l, acc) in zip(chains, carries):
        o_ref[hh * hd:(hh + 1) * hd, u * t:(u + 1) * t] = (acc / l).astype(o_ref.dtype)


def _attn(qT, kp, vT, stats):
    b, nh, _, tq_all = qT.shape
    tk_all = kp.shape[3]
    hd = vT.shape[1] // nh
    t = ATT_T
    nblk = tk_all // t
    q_off = (tk_all - tq_all) // t
    nqb = ATT_QBLOCKS if (tq_all // t) % ATT_QBLOCKS == 0 else 1
    nhs = 1 if nqb > 1 else nh
    ng = nh // nhs
    tq = nqb * t
    head_blk = lambda g, i: (g // ng, g % ng, 0, 0)
    return pl.pallas_call(
        functools.partial(_attn_body, q_off, hd, nqb, nhs),
        grid=(b * ng, tq_all // tq),
        in_specs=[pl.BlockSpec((None, nhs, LANES, tq), lambda g, i: (g // ng, g % ng, 0, i)),
                  pl.BlockSpec((None, nhs, LANES, tk_all), head_blk),
                  pl.BlockSpec((None, nhs * hd, tk_all), lambda g, i: (g // ng, g % ng, 0)),
                  pl.BlockSpec((None, nhs, SUBLANES, nblk), head_blk)],
        out_specs=pl.BlockSpec((None, nhs * hd, tq), lambda g, i: (g // ng, g % ng, i)),
        out_shape=jax.ShapeDtypeStruct((b, nh * hd, tq_all), BF16),
        compiler_params=_cparams(2),
        name="attn",
    )(qT, kp, vT, stats)


def _hgrn_body(nhg, q_ref, lf_ref, k_ref, v_ref, s0_ref, tri_ref, o_ref, sfin_ref,
               st_ref, b_ref, qf_ref, kf_ref, oi_ref, ox_ref, qb_ref, u_ref, dec_ref):
    tt, w = lf_ref.shape
    hd = w // nhg
    cl = HG_CHUNK
    nchunk = tt // cl

    @pl.when(pl.program_id(1) == 0)
    def _():
        for h in range(nhg):
            st_ref[h] = s0_ref[h].T

    tri = tri_ref[...]
    b = jnp.zeros((tt, w), F32)
    for part in _split3(lf_ref[...]):
        b = b + _dot(tri, part)
    qf = q_ref[...].astype(F32)
    kf = k_ref[...].astype(F32)
    for h in range(nhg):
        hs = slice(h * hd, (h + 1) * hd)
        b_ref[h], qf_ref[h], kf_ref[h] = b[:, hs], qf[:, hs], kf[:, hs]

    rid = lax.broadcasted_iota(jnp.int32, (cl, cl), 0)
    cid = lax.broadcasted_iota(jnp.int32, (cl, cl), 1)
    zpad = jnp.zeros((LANES - cl, hd), F32)
    worst = jnp.zeros((1, hd), F32)
    units = [(h, c) for h in range(nhg) for c in range(nchunk)]
    scores = {}
    for h, c in units:
        hs, sl = slice(h * hd, (h + 1) * hd), slice(c * cl, (c + 1) * cl)
        bc, qc, kc = b[sl, hs], qf[sl, hs], kf[sl, hs]
        vc = v_ref[sl, hs]
        r = bc[cl // 2 - 1:cl // 2, :]
        last = bc[cl - 1:cl, :]
        worst = jnp.maximum(worst, jnp.maximum(-r, r - last))
        qe = (qc * jnp.exp(bc - r)).astype(BF16)
        ke = (kc * jnp.exp(r - bc)).astype(BF16)
        scores[h, c] = _dot_nt(qe, ke)
        qb_ref[h, sl, :] = (qc * jnp.exp(bc)).astype(BF16)
        ke2 = jnp.concatenate([kc * jnp.exp(last - bc), zpad], axis=0).astype(BF16)
        vt = jnp.concatenate([vc.astype(F32), zpad], axis=0).T.astype(BF16)
        u_ref[h, c] = _dot(vt, ke2)
        dec_ref[h * nchunk + c:h * nchunk + c + 1, :] = jnp.exp(last)
    for h, c in units:
        hs, sl = slice(h * hd, (h + 1) * hd), slice(c * cl, (c + 1) * cl)
        a = jnp.where(rid >= cid, scores[h, c], 0.0)
        oi_ref[h, sl, :] = _dot(a.astype(BF16), v_ref[sl, hs])

    states = [st_ref[h] for h in range(nhg)]
    for c in range(nchunk):
        sl = slice(c * cl, (c + 1) * cl)
        for h in range(nhg):
            ox_ref[h, sl, :] = _dot_nt(qb_ref[h, sl, :], states[h].astype(BF16))
            i = h * nchunk + c
            states[h] = states[h] * dec_ref[i:i + 1, :] + u_ref[h, c]
    for h in range(nhg):
        st_ref[h] = states[h]

    def row(t, _):
        base = pl.multiple_of((t // cl) * cl, cl)
        spos = base + lax.broadcasted_iota(jnp.int32, (cl, 1), 0)
        for h in range(nhg):
            hs = slice(h * hd, (h + 1) * hd)
            bt = b_ref[h, pl.ds(t, 1), :]
            qt = qf_ref[h, pl.ds(t, 1), :]
            e = jnp.exp(jnp.minimum(bt - b_ref[h, pl.ds(base, cl), :], 0.0))
            wgt = jnp.where(spos <= t, qt * kf_ref[h, pl.ds(base, cl), :] * e, 0.0)
            a = jnp.sum(wgt, axis=1, keepdims=True)
            vch = v_ref[pl.ds(base, cl), hs].astype(F32)
            oi_ref[h, pl.ds(t, 1), :] = jnp.sum(a * vch, axis=0, keepdims=True)
        return 0

    unsafe_rows = jnp.where(jnp.max(worst) <= HG_SAFE_LOG_RANGE, 0, tt)
    lax.fori_loop(0, unsafe_rows, row, 0)
    for h in range(nhg):
        o_ref[:, h * hd:(h + 1) * hd] = (ox_ref[h] + oi_ref[h]).astype(o_ref.dtype)

    @pl.when(pl.program_id(1) == pl.num_programs(1) - 1)
    def _():
        for h in range(nhg):
            sfin_ref[h] = st_ref[h].T


def _hgrn(hq, hlf, hk, hi, s0, t_valid):
    b, _, w = hq.shape
    nhg, hd = s0.shape[1], s0.shape[2]
    tt = min(HG_TT, t_valid)
    cl = HG_CHUNK
    nchunk = tt // cl
    tri = jnp.asarray(np.kron(np.eye(nchunk, dtype=np.float32),
                              np.tril(np.ones((cl, cl), np.float32))), BF16)
    seq = lambda bi, s: (bi, s, 0)
    sidx = lambda bi, s: (bi, 0, 0, 0)
    blk = pl.BlockSpec((None, tt, w), seq)
    sblk = pl.BlockSpec((None, nhg, hd, hd), sidx)
    head_rows = pltpu.VMEM((nhg, tt, hd), F32)
    return pl.pallas_call(
        functools.partial(_hgrn_body, nhg),
        grid=(b, t_valid // tt),
        in_specs=[blk, blk, blk, blk, sblk, pl.BlockSpec(tri.shape, lambda bi, s: (0, 0))],
        out_specs=[blk, sblk],
        out_shape=[jax.ShapeDtypeStruct((b, t_valid, w), BF16),
                   jax.ShapeDtypeStruct((b, nhg, hd, hd), F32)],
        scratch_shapes=[pltpu.VMEM((nhg, hd, hd), F32), head_rows, head_rows, head_rows, head_rows,
                        head_rows, pltpu.VMEM((nhg, tt, hd), BF16),
                        pltpu.VMEM((nhg, nchunk, hd, hd), F32), pltpu.VMEM((nhg * nchunk, hd), F32)],
        compiler_params=_cparams(2),
        name="hgrn",
    )(hq, hlf, hk, hi, s0, tri)


def _merge_body(final, nhg, x_ref, oT_ref, fg_ref, ho_ref, hg_ref, gn_ref, w1_ref, w2_ref, fgain_ref,
                y_ref):
    fy = (oT_ref[...].astype(F32).T * _silu(fg_ref[...].astype(F32))).astype(BF16)
    ho = ho_ref[...].astype(F32)
    hw = ho.shape[1] // nhg
    normed = []
    for h in range(nhg):
        oh = ho[:, h * hw:(h + 1) * hw]
        inv = lax.rsqrt(jnp.mean(oh * oh, axis=-1, keepdims=True) + NORM_EPS)
        normed.append(oh * inv * gn_ref[:, h * hw:(h + 1) * hw])
    hy = (jnp.concatenate(normed, axis=1) * _silu(hg_ref[...].astype(F32))).astype(BF16)
    out = x_ref[...] + _dot(fy, w1_ref[...]) + _dot(hy, w2_ref[...])
    if final:
        inv = lax.rsqrt(jnp.mean(out * out, axis=-1, keepdims=True) + NORM_EPS)
        out = (out * inv) * fgain_ref[...]
    y_ref[...] = out


def _merge(x, oT, fg, ho, hg, gn, w1, w2, fgain, nhg, final):
    b, t, d = x.shape
    fw = fg.shape[2]
    tm = min(MERGE_TM, t)
    assert t % tm == 0
    seq = lambda bi, s: (bi, s, 0)
    const = lambda bi, s: (0, 0)
    blk = pl.BlockSpec((None, tm, fw), seq)
    return pl.pallas_call(
        functools.partial(_merge_body, final, nhg),
        grid=(b, t // tm),
        in_specs=[pl.BlockSpec((None, tm, d), seq),
                  pl.BlockSpec((None, fw, tm), lambda bi, s: (bi, 0, s)),
                  blk, blk, blk, pl.BlockSpec(gn.shape, const),
                  pl.BlockSpec(w1.shape, const), pl.BlockSpec(w2.shape, const),
                  pl.BlockSpec(fgain.shape, const)],
        out_specs=pl.BlockSpec((None, tm, d), seq),
        out_shape=jax.ShapeDtypeStruct((b, t, d), F32),
        compiler_params=_cparams(2),
        name="merge",
    )(x, oT, fg, ho, hg, gn, w1, w2, fgain)


def _path(x, t_valid, cache, states0, wts):
    nh, hd, nhg = wts["nh"], wts["hd"], wts["nhg"]
    depth = wts["hl"].shape[0]
    t = x.shape[1]
    feats = None
    s_fin = []
    for layer in range(depth):
        if cache is None:
            qT, kT, vT, lfT, fg, hq, hlf, hk, hi, hg, kp, vTb, stats = _proj(x, wts, layer, feats, True)
            feats = (kT, vT, lfT)
        else:
            qT, kT, vT, lfT, fg, hq, hlf, hk, hi, hg = _proj(x, wts, layer, feats, False)
            feats = (kT, vT, lfT)
            qT, kp, vTb, stats = _prep(qT, feats, cache, layer, nh, hd)
        stats = jnp.transpose(stats[..., :SUBLANES], (0, 2, 3, 1))
        oT = _attn(qT, kp, vTb, stats)
        ho, s_new = _hgrn(hq, hlf, hk, hi, states0[layer], t_valid)
        if t_valid < t:
            ho = jnp.pad(ho, ((0, 0), (0, t - t_valid), (0, 0)))
        x = _merge(x, oT, fg, ho, hg, wts["gn"][layer], wts["w1"][layer], wts["w2"][layer],
                   wts["fgain"], nhg, layer == depth - 1)
        s_fin.append(s_new)
    return x, feats, jnp.stack(s_fin)


def _caches_out(feats, t_valid, nh, hd):
    kT, vT, lfT = (a[..., :t_valid] for a in feats)
    depth, b = kT.shape[:2]
    to_tok = lambda a: jnp.transpose(a.reshape(depth, b, nh, hd, t_valid), (0, 1, 4, 2, 3))
    return to_tok(kT), to_tok(vT), jnp.transpose(lfT, (0, 1, 3, 2))


def kernel(x_prompt, x_sample, cache_k, cache_v, cache_logf, state_hgrn,
           norm_g, w_in, fox_b_f, hg_lower, hg_norm_g, w_out, final_g):
    depth, d = norm_g.shape
    nh = fox_b_f.shape[1]
    fw = w_out.shape[1] // 2
    hd = fw // nh
    nhg = state_hgrn.shape[2]
    assert hd % SUBLANES == 0 and hd + SUBLANES <= LANES and nh <= SUBLANES
    o = 3 * fw
    wm = jnp.concatenate([w_in[:, :, :o], w_in[:, :, o + nh:]], axis=2).astype(BF16)
    wf = jnp.pad(w_in[:, :, o:o + nh], ((0, 0), (0, 0), (0, LANES - nh))).astype(BF16)
    wts = dict(
        nh=nh, hd=hd, nhg=nhg, wm=wm, wf=wf,
        g=norm_g.reshape(depth, 1, d),
        bf=jnp.pad(fox_b_f, ((0, 0), (0, LANES - nh))).reshape(depth, 1, LANES),
        hl=hg_lower, gn=hg_norm_g.reshape(depth, 1, -1),
        w1=w_out[:, :fw, :].astype(BF16), w2=w_out[:, fw:, :].astype(BF16),
        fgain=final_g.reshape(1, d))

    bp, sp, _ = x_prompt.shape
    assert sp % PROJ_TM == 0 and PROJ_TM % ATT_T == 0 and sp >= 2 * ATT_T
    zero_states = jnp.zeros((depth, bp) + state_hgrn.shape[2:], F32)
    y_p, feats_p, s_p = _path(x_prompt, sp, None, zero_states, wts)

    bs, ts_, _ = x_sample.shape
    past = cache_k.shape[2]
    assert past % ATT_T == 0 and ts_ % HG_CHUNK == 0
    t_pad = -(-ts_ // ATT_T) * ATT_T
    xs = jnp.pad(x_sample, ((0, 0), (0, t_pad - ts_), (0, 0)))
    to_featT = lambda a: jnp.transpose(a, (0, 1, 3, 4, 2)).reshape(depth, bs, fw, past)
    cache = (to_featT(cache_k), to_featT(cache_v), jnp.transpose(cache_logf, (0, 1, 3, 2)))
    y_s, feats_s, s_s = _path(xs, ts_, cache, state_hgrn, wts)

    k_p, v_p, lf_p = _caches_out(feats_p, sp, nh, hd)
    k_s, v_s, lf_s = _caches_out(feats_s, ts_, nh, hd)
    return (y_p, y_s[:, :ts_], k_p, v_p, lf_p, s_p, k_s, v_s, lf_s, s_s)
```
